```python
import math
import jax, jax.numpy as jnp
from jax import lax
import numpy as np

D_MODEL = 4096
BATCH = 1
SEQ = 8192
DEPTH = 4

N_MEM = 256
DN_HEAD_DIM = 128
DN_WIDTH = 3 * D_MODEL // 8
DN_HEADS = DN_WIDTH // DN_HEAD_DIM
CONV_K = 4
CHUNK = 64
DA_HEAD_DIM = 128
DA_V_DIM = 2 * DA_HEAD_DIM
DA_WIDTH = 3 * D_MODEL // 8
DA_HEADS = DA_WIDTH // DA_V_DIM
Q_BLOCK = 128
ROPE_THETA = 10000.0
MEM_HEADS = 4
MEM_WIDTH = D_MODEL - DN_WIDTH - DA_WIDTH
MEM_HEAD_DIM = MEM_WIDTH // MEM_HEADS
IN_SPLIT_SIZES = (DN_WIDTH, DN_WIDTH, DN_WIDTH, DN_WIDTH, DN_HEADS, DN_HEADS,
                  DA_WIDTH, DA_WIDTH, DA_WIDTH, MEM_WIDTH)
IN_WIDTH = sum(IN_SPLIT_SIZES)
N_EXPERTS = 32
TOP_K = 4
D_EXPERT = 256
SWIGLU_LIMIT = 7.0
SWIGLU_ALPHA = 1.702
MOE_BLOCK = 128
DEEPNORM_ALPHA = (2 * DEPTH) ** 0.25
DEEPNORM_BETA = (8 * DEPTH) ** -0.25
LN_EPS = 1e-5
RMS_EPS = 1e-6

kernel_name = 'hybrid_deltanet_diffattn_memxattn_moe_deepnorm'


def layer_norm(x, g, b):
    xf = x.astype(jnp.float32)
    mu = jnp.mean(xf, -1, keepdims=True)
    var = jnp.mean(jnp.square(xf - mu), -1, keepdims=True)
    return ((xf - mu) * lax.rsqrt(var + LN_EPS) * g + b).astype(x.dtype)


def rms_norm(x, w):
    xf = x.astype(jnp.float32)
    return xf * lax.rsqrt(jnp.mean(xf * xf, -1, keepdims=True) + RMS_EPS) * w


def l2_normalize(x):
    return x * lax.rsqrt(jnp.sum(x * x, -1, keepdims=True) + 1e-6)


def rope_tables(positions):
    inv = 1.0 / (ROPE_THETA ** (jnp.arange(0, DA_HEAD_DIM, 2, dtype=jnp.float32) / DA_HEAD_DIM))
    ang = positions.astype(jnp.float32)[..., None] * inv
    ang = jnp.concatenate([ang, ang], -1)
    return jnp.cos(ang), jnp.sin(ang)


def apply_rope(x, cos, sin):
    c = cos[:, None, None]
    s = sin[:, None, None]
    xf = x.astype(jnp.float32)
    x1, x2 = jnp.split(xf, 2, -1)
    rot = jnp.concatenate([-x2, x1], -1)
    return (xf * c + rot * s).astype(x.dtype)


def causal_depthwise_conv_silu(x, w):
    c = x.shape[-1]
    y = lax.conv_general_dilated(
        x, w[:, None, :].astype(x.dtype), window_strides=(1,),
        padding=[(CONV_K - 1, 0)], dimension_numbers=('NWC', 'WIO', 'NWC'),
        feature_group_count=c)
    return jax.nn.silu(y)


def chunked_gated_delta_rule(q, k, v, g, beta):
    bsz, h, s, dk = q.shape
    dv = v.shape[-1]
    n = s // CHUNK
    q, k, v = (t.reshape(bsz, h, n, CHUNK, t.shape[-1]) for t in (q, k, v))
    g = g.reshape(bsz, h, n, CHUNK)
    beta = beta.reshape(bsz, h, n, CHUNK)
    G = jnp.cumsum(g, -1)
    idx = jnp.arange(CHUNK)
    incl = idx[:, None] >= idx[None, :]
    strict = idx[:, None] > idx[None, :]
    decay = jnp.exp(jnp.where(incl, G[..., :, None] - G[..., None, :], -jnp.inf))
    kk = jnp.einsum('bhncd,bhnmd->bhncm', k, k)
    A = jnp.where(strict, beta[..., :, None] * kk * decay, 0.0)
    L = A + jnp.eye(CHUNK, dtype=A.dtype)
    u = lax.linalg.triangular_solve(L, v * beta[..., None], left_side=True,
                                    lower=True, unit_diagonal=True)
    w = lax.linalg.triangular_solve(L, k * (beta * jnp.exp(G))[..., None], left_side=True,
                                    lower=True, unit_diagonal=True)
    qk = jnp.einsum('bhncd,bhnmd->bhncm', q, k) * decay
    q_dec = q * jnp.exp(G)[..., None]
    G_last = G[..., -1]
    k_dec = k * jnp.exp(G_last[..., None] - G)[..., None]

    def step(state, xs):
        u_c, w_c, qk_c, q_c, k_c, gl = xs
        v_new = u_c - jnp.einsum('bhcd,bhde->bhce', w_c, state)
        o = (jnp.einsum('bhcd,bhde->bhce', q_c, state)
             + jnp.einsum('bhcm,bhme->bhce', qk_c, v_new))
        state = state * jnp.exp(gl)[..., None, None] + jnp.einsum('bhcd,bhce->bhde', k_c, v_new)
        return state, o

    xs = tuple(jnp.moveaxis(t, 2, 0) for t in (u, w, qk, q_dec, k_dec, G_last))
    state0 = jnp.zeros((bsz, h, dk, dv), jnp.float32)
    _, o = lax.scan(step, state0, xs)
    return jnp.moveaxis(o, 0, 2).reshape(bsz, h, s, dv)


def gated_deltanet(q, k, v, z, a, b, conv_w, a_log, dt_bias, norm_w):
    bsz, s, _ = q.shape
    out_dtype = q.dtype
    qkv = causal_depthwise_conv_silu(jnp.concatenate([q, k, v], -1), conv_w)
    q, k, v = jnp.split(qkv.astype(jnp.float32), 3, -1)
    heads = lambda t: t.reshape(bsz, s, DN_HEADS, DN_HEAD_DIM).transpose(0, 2, 1, 3)
    q = l2_normalize(heads(q)) * DN_HEAD_DIM ** -0.5
    k = l2_normalize(heads(k))
    v = heads(v)
    g = -jnp.exp(a_log.astype(jnp.float32)) * jax.nn.softplus(a.astype(jnp.float32) + dt_bias.astype(jnp.float32))
    beta = jax.nn.sigmoid(b.astype(jnp.float32))
    o = chunked_gated_delta_rule(q, k, v, g.transpose(0, 2, 1), beta.transpose(0, 2, 1))
    o = o.transpose(0, 2, 1, 3)
    zf = z.astype(jnp.float32).reshape(bsz, s, DN_HEADS, DN_HEAD_DIM)
    o = rms_norm(o, norm_w) * jax.nn.silu(zf)
    return o.reshape(bsz, s, DN_WIDTH).astype(out_dtype)


def diff_attention(q, k, v, cos, sin, lq1, lk1, lq2, lk2, norm_w, lam_init):
    bsz, s, _ = q.shape
    out_dtype = q.dtype
    split = lambda t: t.reshape(bsz, s, DA_HEADS, 2, DA_HEAD_DIM).transpose(0, 2, 3, 1, 4)
    q = apply_rope(split(q), cos, sin)
    k = apply_rope(split(k), cos, sin)
    v = v.reshape(bsz, s, DA_HEADS, DA_V_DIM).transpose(0, 2, 1, 3).astype(jnp.float32)
    lam = (jnp.exp(jnp.sum(lq1.astype(jnp.float32) * lk1.astype(jnp.float32)))
           - jnp.exp(jnp.sum(lq2.astype(jnp.float32) * lk2.astype(jnp.float32))) + lam_init)
    nb = s // Q_BLOCK
    q_blocks = jnp.moveaxis(q.reshape(bsz, DA_HEADS, 2, nb, Q_BLOCK, DA_HEAD_DIM), 3, 0)
    key_pos = jnp.arange(s)
    scale = DA_HEAD_DIM ** -0.5

    def block(args):
        q_blk, blk = args
        sc = jnp.einsum('bhmqd,bhmkd->bhmqk', q_blk, k).astype(jnp.float32) * scale
        q_pos = blk * Q_BLOCK + jnp.arange(Q_BLOCK)
        sc = jnp.where(key_pos[None, :] <= q_pos[:, None], sc, -jnp.inf)
        p = jax.nn.softmax(sc, axis=-1)
        p = p[:, :, 0] - lam * p[:, :, 1]
        return jnp.einsum('bhqk,bhke->bhqe', p, v)

    o = lax.map(block, (q_blocks, jnp.arange(nb)))
    o = jnp.moveaxis(o, 0, 2).reshape(bsz, DA_HEADS, s, DA_V_DIM)
    o = rms_norm(o, norm_w) * (1.0 - lam_init)
    return o.transpose(0, 2, 1, 3).reshape(bsz, s, DA_WIDTH).astype(out_dtype)


def memory_attention(q, mem_kv):
    bsz, s, _ = q.shape
    m = mem_kv.shape[1]
    q = q.reshape(bsz, s, MEM_HEADS, MEM_HEAD_DIM)
    k, v = jnp.split(mem_kv, 2, -1)
    k = k.reshape(bsz, m, MEM_HEADS, MEM_HEAD_DIM)
    v = v.reshape(bsz, m, MEM_HEADS, MEM_HEAD_DIM)
    sc = jnp.einsum('bshd,bmhd->bhsm', q, k).astype(jnp.float32) * MEM_HEAD_DIM ** -0.5
    p = jax.nn.softmax(sc, axis=-1)
    o = jnp.einsum('bhsm,bmhd->bshd', p, v.astype(jnp.float32))
    return o.reshape(bsz, s, MEM_WIDTH).astype(q.dtype)


def moe_ffn(h, router_w, router_b, w_gate, b_gate, w_up, b_up, w_down, b_down):
    bsz, s, d = h.shape
    logits = (h @ router_w).astype(jnp.float32) + router_b.astype(jnp.float32)
    top_val, top_idx = lax.top_k(logits, TOP_K)
    gates = jax.nn.softmax(top_val, axis=-1)
    combine = jnp.einsum('bsk,bske->bse', gates,
                         jax.nn.one_hot(top_idx, N_EXPERTS, dtype=jnp.float32))
    tokens = h.reshape(-1, MOE_BLOCK, d)
    comb = combine.reshape(-1, MOE_BLOCK, N_EXPERTS)

    def block(args):
        t, c = args
        gate = jnp.einsum('td,edf->tef', t, w_gate) + b_gate
        up = jnp.einsum('td,edf->tef', t, w_up) + b_up
        gate = jnp.minimum(gate, SWIGLU_LIMIT)
        up = jnp.clip(up, -SWIGLU_LIMIT, SWIGLU_LIMIT)
        act = (up + 1.0) * gate * jax.nn.sigmoid(SWIGLU_ALPHA * gate)
        act = act * c[..., None]
        return jnp.einsum('tef,efd->td', act, w_down) + c @ b_down.astype(jnp.float32)

    y = lax.map(block, (tokens, comb))
    return y.reshape(bsz, s, d).astype(h.dtype)


def lambda_init(layer):
    return 0.8 - 0.6 * math.exp(-0.3 * layer)


def setup_inputs(seed: int = 0) -> dict:
    key = jax.random.key(seed)
    ks = jax.random.split(key, 32)
    nrm = lambda k, shape, scale: jax.random.normal(k, shape, jnp.float32) * scale
    L, D, E, F = DEPTH, D_MODEL, N_EXPERTS, D_EXPERT
    dt = jnp.exp(jax.random.uniform(ks[5], (L, DN_HEADS), jnp.float32,
                                    minval=math.log(1e-3), maxval=math.log(1e-1)))
    return {
        'x': nrm(ks[0], (BATCH, SEQ, D), 1.0),
        'mem': nrm(ks[1], (BATCH, N_MEM, D), 1.0),
        'positions': jnp.broadcast_to(jnp.arange(SEQ, dtype=jnp.int32), (BATCH, SEQ)),
        'w_in': nrm(ks[2], (L, D, IN_WIDTH), D ** -0.5),
        'conv_w': nrm(ks[3], (L, CONV_K, 3 * DN_WIDTH), CONV_K ** -0.5),
        'a_log': jnp.log(jax.random.uniform(ks[4], (L, DN_HEADS), jnp.float32, minval=1.0, maxval=16.0)),
        'dt_bias': dt + jnp.log(-jnp.expm1(-dt)),
        'dn_norm_w': 1.0 + nrm(ks[6], (L, DN_HEAD_DIM), 0.02),
        'lambda_q1': nrm(ks[7], (L, DA_HEAD_DIM), 0.1),
        'lambda_k1': nrm(ks[8], (L, DA_HEAD_DIM), 0.1),
        'lambda_q2': nrm(ks[9], (L, DA_HEAD_DIM), 0.1),
        'lambda_k2': nrm(ks[10], (L, DA_HEAD_DIM), 0.1),
        'da_norm_w': 1.0 + nrm(ks[11], (L, DA_V_DIM), 0.02),
        'w_mem_kv': nrm(ks[12], (L, D, 2 * MEM_WIDTH), D ** -0.5),
        'w_o': nrm(ks[13], (L, D, D), D ** -0.5 * DEEPNORM_BETA),
        'ln1_g': 1.0 + nrm(ks[14], (L, D), 0.02),
        'ln1_b': nrm(ks[15], (L, D), 0.02),
        'router_w': nrm(ks[16], (L, D, E), D ** -0.5),
        'router_b': nrm(ks[17], (L, E), 0.01),
        'w_gate': nrm(ks[18], (L, E, D, F), D ** -0.5),
        'b_gate': nrm(ks[19], (L, E, F), 0.02),
        'w_up': nrm(ks[20], (L, E, D, F), D ** -0.5),
        'b_up': nrm(ks[21], (L, E, F), 0.02),
        'w_down': nrm(ks[22], (L, E, F, D), F ** -0.5 * DEEPNORM_BETA),
        'b_down': nrm(ks[23], (L, E, D), 0.02),
        'ln2_g': 1.0 + nrm(ks[24], (L, D), 0.02),
        'ln2_b': nrm(ks[25], (L, D), 0.02),
    }


def reference(x, mem, positions, w_in, conv_w, a_log, dt_bias, dn_norm_w,
              lambda_q1, lambda_k1, lambda_q2, lambda_k2, da_norm_w, w_mem_kv, w_o,
              ln1_g, ln1_b, router_w, router_b, w_gate, b_gate, w_up, b_up,
              w_down, b_down, ln2_g, ln2_b):
    cos, sin = rope_tables(positions)
    split_points = np.cumsum(IN_SPLIT_SIZES)[:-1].tolist()
    for l in range(DEPTH):
        proj = x @ w_in[l]
        dq, dk, dv, dz, da, db, aq, ak, av, mq = jnp.split(proj, split_points, axis=-1)
        y_dn = gated_deltanet(dq, dk, dv, dz, da, db, conv_w[l], a_log[l], dt_bias[l], dn_norm_w[l])
        y_da = diff_attention(aq, ak, av, cos, sin, lambda_q1[l], lambda_k1[l],
                              lambda_q2[l], lambda_k2[l], da_norm_w[l], lambda_init(l))
        y_mem = memory_attention(mq, mem @ w_mem_kv[l])
        mixed = jnp.concatenate([y_dn, y_da, y_mem], -1) @ w_o[l]
        x = layer_norm(DEEPNORM_ALPHA * x + mixed, ln1_g[l], ln1_b[l])
        ffn = moe_ffn(x, router_w[l], router_b[l], w_gate[l], b_gate[l], w_up[l], b_up[l],
                      w_down[l], b_down[l])
        x = layer_norm(DEEPNORM_ALPHA * x + ffn, ln2_g[l], ln2_b[l])
    return x
```

```python
import functools
import math

import jax
import jax.numpy as jnp
from jax import lax
from jax.experimental import pallas as pl
from jax.experimental.pallas import tpu as pltpu

F32 = jnp.float32
BF16 = jnp.bfloat16
I32 = jnp.int32
HIGHEST = lax.Precision.HIGHEST

V7X_LANES = 128
V7X_SUBLANES = 8
V7X_VMEM_BYTES = 64 * 1024 * 1024
VMEM_COMPILER_MARGIN = 6 * 1024 * 1024

DN_HEAD_DIM = 128
CONV_K = 4
DA_HEAD_DIM = 128
DA_V_DIM = 2 * DA_HEAD_DIM
MEM_HEADS = 4
TOP_K = 4
ROPE_THETA = 10000.0
SWIGLU_LIMIT = 7.0
SWIGLU_ALPHA = 1.702
LN_EPS = 1e-5
RMS_EPS = 1e-6
L2_EPS = 1e-6
LOG2E = 1.4426950408889634

DN_CHUNK = 128
DN_HEADS_PER_STEP = 4
DN_INV_BASE = 16
MOE_TILE = 256


def _vmem_limit(nbytes):
    return int(min(nbytes + VMEM_COMPILER_MARGIN, V7X_VMEM_BYTES - 2 * 1024 * 1024))


def _params(semantics, vmem_bytes):
    return pltpu.CompilerParams(dimension_semantics=semantics, vmem_limit_bytes=_vmem_limit(vmem_bytes))


def _cast_kernel(x_ref, o_ref):
    o_ref[...] = x_ref[...].astype(o_ref.dtype)


def cast_bf16(x, bt=256):
    m, d = x.shape
    bt = min(bt, m)
    return pl.pallas_call(
        _cast_kernel,
        out_shape=jax.ShapeDtypeStruct((m, d), BF16),
        grid=(m // bt,),
        in_specs=[pl.BlockSpec((bt, d), lambda i: (i, 0))],
        out_specs=pl.BlockSpec((bt, d), lambda i: (i, 0)),
        compiler_params=_params(("parallel",), 2 * bt * d * 6),
        name="cast_bf16",
    )(x)


def _mm_kernel(*refs, k_splits):
    n_a = len(k_splits)
    a_refs, w_ref, o_ref, wb_ref = refs[:n_a], refs[n_a], refs[n_a + 1], refs[n_a + 2]

    @pl.when(pl.program_id(1) == 0)
    def _cast_weights():
        wb_ref[...] = w_ref[...].astype(BF16)

    acc = None
    k0 = 0
    for a_ref, kk in zip(a_refs, k_splits):
        part = jnp.dot(a_ref[...], wb_ref[k0:k0 + kk, :], preferred_element_type=F32)
        acc = part if acc is None else acc + part
        k0 += kk
    o_ref[...] = acc.astype(o_ref.dtype)


def matmul(a_list, w, layer, ncols, *, col_block0=0, bm=1024, bn=512, out_dtype=F32, name="proj"):
    m = a_list[0].shape[0]
    k_splits = tuple(a.shape[1] for a in a_list)
    k = sum(k_splits)
    assert w.shape[1] == k and ncols % bn == 0
    bm = min(bm, m)
    assert m % bm == 0
    in_specs = [pl.BlockSpec((bm, kk), lambda j, i: (i, 0)) for kk in k_splits]
    in_specs.append(pl.BlockSpec((None, k, bn), lambda j, i: (layer, 0, j + col_block0)))
    vmem = 2 * bm * k * 2 + 2 * k * bn * 4 + k * bn * 2 + 2 * bm * bn * 4
    return pl.pallas_call(
        functools.partial(_mm_kernel, k_splits=k_splits),
        out_shape=jax.ShapeDtypeStruct((m, ncols), out_dtype),
        grid=(ncols // bn, m // bm),
        in_specs=in_specs,
        out_specs=pl.BlockSpec((bm, bn), lambda j, i: (i, j)),
        scratch_shapes=[pltpu.VMEM((k, bn), BF16)],
        compiler_params=_params(("arbitrary", "arbitrary"), vmem),
        name=name,
    )(*a_list, w)


def _layer_norm_rows(h, g, b):
    mu = jnp.mean(h, -1, keepdims=True)
    d = h - mu
    var = jnp.mean(d * d, -1, keepdims=True)
    return d * lax.rsqrt(var + LN_EPS) * g + b


def _ln_kernel(x_ref, y_ref, g_ref, b_ref, o_ref, ob_ref, *, alpha):
    o = _layer_norm_rows(alpha * x_ref[...] + y_ref[...], g_ref[...], b_ref[...])
    o_ref[...] = o
    ob_ref[...] = o.astype(BF16)


def residual_layer_norm(x, y, g, b, layer, alpha, bt=256):
    s, d = x.shape
    bt = min(bt, s)
    row = pl.BlockSpec((bt, d), lambda i: (i, 0))
    par = pl.BlockSpec((None, 1, d), lambda i: (layer, 0, 0))
    return pl.pallas_call(
        functools.partial(_ln_kernel, alpha=alpha),
        out_shape=(jax.ShapeDtypeStruct((s, d), F32), jax.ShapeDtypeStruct((s, d), BF16)),
        grid=(s // bt,),
        in_specs=[row, row, par, par],
        out_specs=(row, row),
        compiler_params=_params(("parallel",), 2 * bt * d * (4 + 4 + 4 + 2)),
        name="residual_ln",
    )(x, y, g, b)


def _dn_prep_kernel(x_ref, halo_ref, w_ref, o_ref, xs_ref, *, bt, n_q_blocks, n_qk_blocks, q_scale):
    i = pl.program_id(0)
    j = pl.program_id(1)
    halo = halo_ref[...]
    xs_ref[0:V7X_SUBLANES, :] = jnp.where(i == 0, jnp.zeros_like(halo), halo)
    xs_ref[V7X_SUBLANES:, :] = x_ref[...]
    w = w_ref[...]
    y = None
    for tap in range(CONV_K):
        start = V7X_SUBLANES - (CONV_K - 1) + tap
        term = xs_ref[start:start + bt, :] * w[tap:tap + 1, :]
        y = term if y is None else y + term
    y = y * jax.nn.sigmoid(y)
    bc = y.shape[1]
    scale = jnp.where(j < n_q_blocks, q_scale, 1.0).astype(F32)
    for c in range(bc // DN_HEAD_DIM):
        sl = slice(c * DN_HEAD_DIM, (c + 1) * DN_HEAD_DIM)
        yh = y[:, sl]
        nrm = yh * (lax.rsqrt(jnp.sum(yh * yh, -1, keepdims=True) + L2_EPS) * scale)
        o_ref[:, sl] = jnp.where(j < n_qk_blocks, nrm, yh).astype(BF16)


def dn_prep(p1, conv_w, layer, dn_width, bt=256, bc=512):
    s = p1.shape[0]
    ncols = 3 * dn_width
    bt = min(bt, s)
    assert dn_width % bc == 0 and s % bt == 0
    hb = bt // V7X_SUBLANES
    return pl.pallas_call(
        functools.partial(_dn_prep_kernel, bt=bt, n_q_blocks=dn_width // bc, n_qk_blocks=2 * dn_width // bc,
                          q_scale=DN_HEAD_DIM ** -0.5),
        out_shape=jax.ShapeDtypeStruct((s, ncols), BF16),
        grid=(s // bt, ncols // bc),
        in_specs=[
            pl.BlockSpec((bt, bc), lambda i, j: (i, j)),
            pl.BlockSpec((V7X_SUBLANES, bc), lambda i, j: (jnp.maximum(i * hb - 1, 0), j)),
            pl.BlockSpec((None, CONV_K, bc), lambda i, j: (layer, 0, j)),
        ],
        out_specs=pl.BlockSpec((bt, bc), lambda i, j: (i, j)),
        scratch_shapes=[pltpu.VMEM((bt + V7X_SUBLANES, bc), F32)],
        compiler_params=_params(("parallel", "parallel"), 4 * bt * bc * 4),
        name="dn_prep",
    )(p1, p1, conv_w)


def _softplus(x):
    return jnp.maximum(x, 0.0) + jnp.log1p(jnp.exp(-jnp.abs(x)))


def _unit_lower_inverse(a, row, col):
    c = a.shape[0]
    eye = (row == col).astype(F32)
    blk = (row // DN_INV_BASE) == (col // DN_INV_BASE)
    n = jnp.where(blk, -a, 0.0)
    t = eye + n
    p = n.astype(BF16)
    steps = int(math.log2(DN_INV_BASE)) - 1
    for _ in range(steps):
        p32 = jnp.dot(p, p, preferred_element_type=F32)
        p = p32.astype(BF16)
        t = t + jnp.dot(t.astype(BF16), p, preferred_element_type=F32)
    b = DN_INV_BASE
    while b < c:
        pair = ((row // (2 * b)) == (col // (2 * b))) & ((row // b) != (col // b))
        mb = jnp.where(pair, a, 0.0).astype(BF16)
        tb = t.astype(BF16)
        tm = jnp.dot(tb, mb, preferred_element_type=F32).astype(BF16)
        t = t - jnp.dot(tm, tb, preferred_element_type=F32)
        b *= 2
    return t


def _dn_kernel(q_ref, k_ref, v_ref, z_ref, ac_ref, bc_ref, ar_ref, br_ref, alc_ref, dtc_ref, alr_ref, dtr_ref,
               nw_ref, o_ref, state_ref, *, hb, chunk, n_chunks):
    c_ = chunk
    d_ = DN_HEAD_DIM

    @pl.when(pl.program_id(1) == 0)
    def _reset():
        state_ref[...] = jnp.zeros_like(state_ref)

    row = lax.broadcasted_iota(I32, (c_, c_), 0)
    col = lax.broadcasted_iota(I32, (c_, c_), 1)
    incl = row >= col
    strict = row > col
    ltri = incl.astype(F32)
    neg_decay_c = -jnp.exp(alc_ref[...])
    neg_decay_r = -jnp.exp(alr_ref[...])
    nw = nw_ref[...]
    contract_last = (((1,), (1,)), ((), ()))
    contract_first = (((0,), (0,)), ((), ()))

    def chunk_body(ci, carry):
        r0 = pl.multiple_of(ci * c_, c_)
        rows = pl.ds(r0, c_)
        g_c = neg_decay_c * _softplus(ac_ref[rows, :] + dtc_ref[...])
        beta_c = jax.nn.sigmoid(bc_ref[rows, :])
        cum_c = jnp.dot(ltri, g_c, precision=HIGHEST, preferred_element_type=F32)
        g_r = neg_decay_r * _softplus(ar_ref[ci] + dtr_ref[...])
        cum_r = lax.dot_general(g_r, ltri, contract_last, precision=HIGHEST, preferred_element_type=F32)
        for hh in range(hb):
            cols = slice(hh * d_, (hh + 1) * d_)
            gc = cum_c[:, hh:hh + 1]
            gr = cum_r[hh:hh + 1, :]
            bcol = beta_c[:, hh:hh + 1]
            decay = jnp.exp(jnp.where(incl, gc - gr, -jnp.inf))
            qh = q_ref[rows, cols]
            kh = k_ref[rows, cols]
            vh = v_ref[rows, cols]
            kk = lax.dot_general(kh, kh, contract_last, preferred_element_type=F32)
            qk = lax.dot_general(qh, kh, contract_last, preferred_element_type=F32)
            a = jnp.where(strict, bcol * kk * decay, 0.0)
            t = _unit_lower_inverse(a, row, col)
            e_g = jnp.exp(gc)
            kf = kh.astype(F32)
            rhs = jnp.concatenate([vh.astype(F32) * bcol, kf * (bcol * e_g)], axis=1).astype(BF16)
            uw = jnp.dot(t.astype(BF16), rhs, preferred_element_type=F32)
            u = uw[:, :d_]
            w = uw[:, d_:]
            state = state_ref[hh]
            q_dec = (qh.astype(F32) * e_g).astype(BF16)
            wq = jnp.concatenate([w.astype(BF16), q_dec], axis=0)
            ws = jnp.dot(wq, state.astype(BF16), preferred_element_type=F32)
            v_new = u - ws[:c_]
            v_new_b = v_new.astype(BF16)
            o = ws[c_:] + jnp.dot((qk * decay).astype(BF16), v_new_b, preferred_element_type=F32)
            g_last = gc[c_ - 1:c_, :]
            k_dec = (kf * jnp.exp(g_last - gc)).astype(BF16)
            state_ref[hh] = state * jnp.exp(g_last) + lax.dot_general(
                k_dec, v_new_b, contract_first, preferred_element_type=F32)
            zf = z_ref[rows, cols]
            on = o * lax.rsqrt(jnp.mean(o * o, -1, keepdims=True) + RMS_EPS) * nw
            o_ref[rows, cols] = (on * (zf * jax.nn.sigmoid(zf))).astype(BF16)
        return carry

    lax.fori_loop(0, n_chunks, chunk_body, 0)


def deltanet(qkv, p1, gates_col, gates_row, a_log, dt_bias, norm_w, layer, dn_width, bs=1024):
    s = qkv.shape[0]
    hb = DN_HEADS_PER_STEP
    c_ = DN_CHUNK
    heads = dn_width // DN_HEAD_DIM
    assert heads % hb == 0
    groups = heads // hb
    bs = min(bs, s)
    assert s % bs == 0 and bs % c_ == 0
    wblk = hb * DN_HEAD_DIM
    ncb = dn_width // wblk
    n_chunks = bs // c_
    a_log_c = a_log.reshape(-1, groups, 1, hb)
    a_log_r = a_log.reshape(-1, groups, hb, 1)
    dt_c = dt_bias.reshape(-1, groups, 1, hb)
    dt_r = dt_bias.reshape(-1, groups, hb, 1)
    nw = norm_w.reshape(-1, 1, DN_HEAD_DIM)

    def seg(which):
        return pl.BlockSpec((bs, wblk), lambda g, i: (i, which * ncb + g))

    col_spec = lambda w: pl.BlockSpec((None, None, bs, hb), lambda g, i: (w, g, i, 0))
    row_spec = lambda w: pl.BlockSpec((None, None, n_chunks, hb, c_), lambda g, i: (w, g, i, 0, 0))
    pc_spec = pl.BlockSpec((None, None, 1, hb), lambda g, i: (layer, g, 0, 0))
    pr_spec = pl.BlockSpec((None, None, hb, 1), lambda g, i: (layer, g, 0, 0))
    vmem = 2 * bs * wblk * (2 * 3 + 4 + 2) + hb * DN_HEAD_DIM * DN_HEAD_DIM * 4 + 16 * bs * V7X_LANES * 4
    return pl.pallas_call(
        functools.partial(_dn_kernel, hb=hb, chunk=c_, n_chunks=n_chunks),
        out_shape=jax.ShapeDtypeStruct((s, dn_width), BF16),
        grid=(groups, s // bs),
        in_specs=[seg(0), seg(1), seg(2), seg(3),
                  col_spec(0), col_spec(1), row_spec(0), row_spec(1),
                  pc_spec, pc_spec, pr_spec, pr_spec,
                  pl.BlockSpec((None, 1, DN_HEAD_DIM), lambda g, i: (layer, 0, 0))],
        out_specs=pl.BlockSpec((bs, wblk), lambda g, i: (i, g)),
        scratch_shapes=[pltpu.VMEM((hb, DN_HEAD_DIM, DN_HEAD_DIM), F32)],
        compiler_params=_params(("parallel", "arbitrary"), vmem),
        name="deltanet",
    )(qkv, qkv, qkv, p1, gates_col, gates_col, gates_row, gates_row, a_log_c, dt_c, a_log_r, dt_r, nw)


def _da_prep_kernel(q_ref, k_ref, v_ref, cos_ref, sin_ref, qo_ref, ko_ref, vo_ref, *, q_scale):
    c = cos_ref[...]
    s = sin_ref[...]
    half = DA_HEAD_DIM // 2
    for j in range(q_ref.shape[1] // DA_HEAD_DIM):
        sl = slice(j * DA_HEAD_DIM, (j + 1) * DA_HEAD_DIM)
        xq = q_ref[:, sl]
        xk = k_ref[:, sl]
        qo_ref[:, sl] = ((xq * c + pltpu.roll(xq, half, 1) * s) * q_scale).astype(BF16)
        ko_ref[:, sl] = (xk * c + pltpu.roll(xk, half, 1) * s).astype(BF16)
    vo_ref[...] = v_ref[...].astype(BF16)


def da_prep(p2, cos, sin_signed, da_width, bt=256):
    s = p2.shape[0]
    bt = min(bt, s)
    seg = lambda w: pl.BlockSpec((bt, da_width), lambda i: (i, w))
    tab = pl.BlockSpec((bt, DA_HEAD_DIM), lambda i: (i, 0))
    out = jax.ShapeDtypeStruct((s, da_width), BF16)
    return pl.pallas_call(
        functools.partial(_da_prep_kernel, q_scale=DA_HEAD_DIM ** -0.5 * LOG2E),
        out_shape=(out, out, out),
        grid=(s // bt,),
        in_specs=[seg(0), seg(1), seg(2), tab, tab],
        out_specs=(seg(0), seg(0), seg(0)),
        compiler_params=_params(("parallel",), 2 * 3 * bt * da_width * 6),
        name="da_prep",
    )(p2, p2, p2, cos, sin_signed)


def _da_kernel(lam_init_ref, q_ref, k_ref, v_ref, lq1_ref, lk1_ref, lq2_ref, lk2_ref, nw_ref, o_ref,
               acc_ref, m_ref, l_ref, *, blk):
    iq = pl.program_id(1)
    d_ = DA_HEAD_DIM
    contract_last = (((1,), (1,)), ((), ()))
    m_ref[...] = jnp.full_like(m_ref, -jnp.inf)
    l_ref[...] = jnp.zeros_like(l_ref)
    acc_ref[...] = jnp.zeros_like(acc_ref)
    q_sub = [q_ref[:, 0:d_], q_ref[:, d_:2 * d_]]

    def kv_step(j, masked):
        r0 = pl.multiple_of(j * blk, blk)
        kb = k_ref[pl.ds(r0, blk), :]
        vb = v_ref[pl.ds(r0, blk), :]
        for sub in range(2):
            s = lax.dot_general(q_sub[sub], kb[:, sub * d_:(sub + 1) * d_], contract_last,
                                preferred_element_type=F32)
            if masked:
                row = lax.broadcasted_iota(I32, s.shape, 0)
                col = lax.broadcasted_iota(I32, s.shape, 1)
                s = jnp.where(col <= row, s, -jnp.inf)
            m_old = m_ref[sub]
            m_new = jnp.maximum(m_old, jnp.max(s, -1, keepdims=True))
            alpha = jnp.exp2(m_old - m_new)
            p = jnp.exp2(s - m_new)
            l_ref[sub] = alpha * l_ref[sub] + jnp.sum(p, -1, keepdims=True)
            acc_ref[sub] = alpha * acc_ref[sub] + jnp.dot(p.astype(BF16), vb, preferred_element_type=F32)
            m_ref[sub] = m_new

    def body(j, carry):
        kv_step(j, False)
        return carry

    lax.fori_loop(0, iq, body, 0)
    kv_step(iq, True)

    lam_init = lam_init_ref[0]
    lam = (jnp.exp(jnp.sum(lq1_ref[...] * lk1_ref[...], keepdims=True))
           - jnp.exp(jnp.sum(lq2_ref[...] * lk2_ref[...], keepdims=True)) + lam_init)
    o = acc_ref[0] / l_ref[0] - lam * (acc_ref[1] / l_ref[1])
    on = o * lax.rsqrt(jnp.mean(o * o, -1, keepdims=True) + RMS_EPS) * nw_ref[...]
    o_ref[...] = (on * (1.0 - lam_init)).astype(BF16)


def diff_attention(qr, kr, vb, lq1, lk1, lq2, lk2, norm_w, lam_init, layer, blk=512):
    s, width = qr.shape
    heads = width // DA_V_DIM
    blk = min(blk, s)
    assert s % blk == 0
    vec = pl.BlockSpec((None, 1, DA_HEAD_DIM), lambda h, i, *_: (layer, 0, 0))
    full = pl.BlockSpec((s, DA_V_DIM), lambda h, i, *_: (0, h))
    tile = pl.BlockSpec((blk, DA_V_DIM), lambda h, i, *_: (i, h))
    grid_spec = pltpu.PrefetchScalarGridSpec(
        num_scalar_prefetch=1,
        grid=(heads, s // blk),
        in_specs=[tile, full, full, vec, vec, vec, vec,
                  pl.BlockSpec((None, 1, DA_V_DIM), lambda h, i, *_: (layer, 0, 0))],
        out_specs=tile,
        scratch_shapes=[pltpu.VMEM((2, blk, DA_V_DIM), F32), pltpu.VMEM((2, blk, 1), F32),
                        pltpu.VMEM((2, blk, 1), F32)],
    )
    vmem = 4 * s * DA_V_DIM * 2 + 4 * blk * DA_V_DIM * 2 + 2 * blk * DA_V_DIM * 4 + 8 * blk * blk * 4
    return pl.pallas_call(
        functools.partial(_da_kernel, blk=blk),
        out_shape=jax.ShapeDtypeStruct((s, width), BF16),
        grid_spec=grid_spec,
        compiler_params=_params(("parallel", "arbitrary"), vmem),
        name="diff_attention",
    )(lam_init, qr, kr, vb, lq1, lk1, lq2, lk2, norm_w)


def _mem_attn_kernel(q_ref, k_ref, v_ref, o_ref, *, scale):
    q = (q_ref[...] * scale).astype(BF16)
    k = k_ref[...].astype(BF16)
    sc = lax.dot_general(q, k, (((1,), (1,)), ((), ())), preferred_element_type=F32)
    sc = sc - jnp.max(sc, -1, keepdims=True)
    p = jnp.exp(sc)
    denom = jnp.sum(p, -1, keepdims=True)
    o = jnp.dot(p.astype(BF16), v_ref[...].astype(BF16), preferred_element_type=F32) / denom
    o_ref[...] = o.astype(BF16)


def memory_attention(p2, mem_kv, q_col0, mem_width, bt=512):
    s = p2.shape[0]
    n_mem = mem_kv.shape[0]
    hd = mem_width // MEM_HEADS
    bt = min(bt, s)
    assert q_col0 % hd == 0
    qb0 = q_col0 // hd
    return pl.pallas_call(
        functools.partial(_mem_attn_kernel, scale=hd ** -0.5),
        out_shape=jax.ShapeDtypeStruct((s, mem_width), BF16),
        grid=(s // bt, MEM_HEADS),
        in_specs=[pl.BlockSpec((bt, hd), lambda i, h: (i, qb0 + h)),
                  pl.BlockSpec((n_mem, hd), lambda i, h: (0, h)),
                  pl.BlockSpec((n_mem, hd), lambda i, h: (0, MEM_HEADS + h))],
        out_specs=pl.BlockSpec((bt, hd), lambda i, h: (i, h)),
        compiler_params=_params(("parallel", "parallel"), 4 * bt * hd * 4 + 4 * n_mem * hd * 4 + 4 * bt * n_mem * 4),
        name="memory_attention",
    )(p2, mem_kv, mem_kv)


def _router_kernel(x_ref, w_ref, b_ref, idx_ref, gate_ref, rank_ref, cnt_ref, carry_ref):
    i = pl.program_id(0)

    @pl.when(i == 0)
    def _reset():
        carry_ref[...] = jnp.zeros_like(carry_ref)

    logits = jnp.dot(x_ref[...], w_ref[...], precision=HIGHEST, preferred_element_type=F32) + b_ref[...]
    bt, n_e = logits.shape
    lane = lax.broadcasted_iota(I32, (bt, n_e), 1)
    work = logits
    vals, sels = [], []
    for kk in range(TOP_K):
        mx = jnp.max(work, -1, keepdims=True)
        idx = jnp.min(jnp.where(work == mx, lane, n_e), -1, keepdims=True)
        sel = lane == idx
        vals.append(mx)
        sels.append(sel)
        idx_ref[:, kk:kk + 1] = idx
        work = jnp.where(sel, -jnp.inf, work)
    exps = [jnp.exp(v - vals[0]) for v in vals]
    denom = exps[0]
    for e in exps[1:]:
        denom = denom + e
    for kk in range(TOP_K):
        gate_ref[:, kk:kk + 1] = exps[kk] / denom
    mask = sels[0]
    for sel in sels[1:]:
        mask = mask | sel
    maskf = mask.astype(F32)
    row = lax.broadcasted_iota(I32, (bt, bt), 0)
    col = lax.broadcasted_iota(I32, (bt, bt), 1)
    before = (row > col).astype(BF16)
    rank = carry_ref[...] + jnp.dot(before, maskf.astype(BF16), preferred_element_type=F32)
    for kk in range(TOP_K):
        rank_ref[:, kk:kk + 1] = jnp.sum(jnp.where(sels[kk], rank, 0.0), -1, keepdims=True).astype(I32)
    carry_ref[...] = carry_ref[...] + jnp.sum(maskf, 0, keepdims=True)
    cnt_ref[...] = carry_ref[...].astype(I32)


def router(x, router_w, router_b, layer, bt=256):
    s, d = x.shape
    n_e = router_w.shape[-1]
    bt = min(bt, s)
    small = lambda dt: jax.ShapeDtypeStruct((s, TOP_K), dt)
    tok = pl.BlockSpec((bt, TOP_K), lambda i: (i, 0))
    return pl.pallas_call(
        _router_kernel,
        out_shape=(small(I32), small(F32), small(I32), jax.ShapeDtypeStruct((1, n_e), I32)),
        grid=(s // bt,),
        in_specs=[pl.BlockSpec((bt, d), lambda i: (i, 0)),
                  pl.BlockSpec((None, d, n_e), lambda i: (layer, 0, 0)),
                  pl.BlockSpec((None, 1, n_e), lambda i: (layer, 0, 0))],
        out_specs=(tok, tok, tok, pl.BlockSpec((1, n_e), lambda i: (0, 0))),
        scratch_shapes=[pltpu.VMEM((1, n_e), F32)],
        compiler_params=_params(("arbitrary",), 2 * bt * d * 4 + 2 * d * V7X_LANES * 4),
        name="router",
    )(x, router_w, router_b)


def _row_copy(src, src_row, dst, dst_row, sem):
    return pltpu.make_async_copy(src.at[pl.ds(src_row, 1)], dst.at[pl.ds(dst_row, 1)], sem)


def _moe_kernel(te_ref, tr_ref, nt_ref, src_ref, dst_ref, x_hbm, wg_ref, wu_ref, wd_ref, bg_ref, bu_ref, bd_ref,
                y_hbm, xbuf, ybuf, wgb, wub, wdb, sems, *, tm):
    i = pl.program_id(0)

    @pl.when(i < nt_ref[0])
    def _tile():
        n_rows = tr_ref[i]

        def gather(r, c):
            _row_copy(x_hbm, src_ref[0, r], xbuf, r, sems.at[0]).start()
            return c

        lax.fori_loop(0, tm, gather, 0)
        prev = te_ref[jnp.maximum(i - 1, 0)]

        @pl.when((i == 0) | (te_ref[i] != prev))
        def _cast_weights():
            wgb[...] = wg_ref[...].astype(BF16)
            wub[...] = wu_ref[...].astype(BF16)
            wdb[...] = wd_ref[...].astype(BF16)

        def gather_wait(r, c):
            _row_copy(x_hbm, 0, xbuf, r, sems.at[0]).wait()
            return c

        lax.fori_loop(0, tm, gather_wait, 0)
        xb = xbuf[...].astype(BF16)
        gate = jnp.dot(xb, wgb[...], preferred_element_type=F32) + bg_ref[...]
        up = jnp.dot(xb, wub[...], preferred_element_type=F32) + bu_ref[...]
        gate = jnp.minimum(gate, SWIGLU_LIMIT)
        up = jnp.clip(up, -SWIGLU_LIMIT, SWIGLU_LIMIT)
        act = (up + 1.0) * gate * jax.nn.sigmoid(SWIGLU_ALPHA * gate)
        ybuf[...] = jnp.dot(act.astype(BF16), wdb[...], preferred_element_type=F32) + bd_ref[...]

        def scatter(r, c):
            _row_copy(ybuf, r, y_hbm, dst_ref[0, r], sems.at[1]).start()
            return c

        lax.fori_loop(0, n_rows, scatter, 0)

        def scatter_wait(r, c):
            _row_copy(ybuf, r, y_hbm, 0, sems.at[1]).wait()
            return c

        lax.fori_loop(0, n_rows, scatter_wait, 0)


def routed_experts(x, tile_expert, tile_rows, n_tiles, src_tok, dst_row, w_gate, b_gate, w_up, b_up, w_down,
                   b_down, layer):
    s, d = x.shape
    f = w_gate.shape[-1]
    tm = MOE_TILE
    nt = tile_expert.shape[0]
    idx_spec = pl.BlockSpec((None, 1, tm), lambda i, *_: (i, 0, 0), memory_space=pltpu.SMEM)
    wspec = lambda a, b: pl.BlockSpec((None, None, a, b), lambda i, te, tr, n: (layer, te[i], 0, 0))
    grid_spec = pltpu.PrefetchScalarGridSpec(
        num_scalar_prefetch=3,
        grid=(nt,),
        in_specs=[idx_spec, idx_spec, pl.BlockSpec(memory_space=pl.ANY),
                  wspec(d, f), wspec(d, f), wspec(f, d), wspec(1, f), wspec(1, f), wspec(1, d)],
        out_specs=pl.BlockSpec(memory_space=pl.ANY),
        scratch_shapes=[pltpu.VMEM((tm, d), F32), pltpu.VMEM((tm, d), F32),
                        pltpu.VMEM((d, f), BF16), pltpu.VMEM((d, f), BF16), pltpu.VMEM((f, d), BF16),
                        pltpu.SemaphoreType.DMA((2,))],
    )
    vmem = 2 * 3 * d * f * 4 + 3 * d * f * 2 + 4 * tm * d * 4
    return pl.pallas_call(
        functools.partial(_moe_kernel, tm=tm),
        out_shape=jax.ShapeDtypeStruct((TOP_K * s, d), F32),
        grid_spec=grid_spec,
        compiler_params=_params(("arbitrary",), vmem),
        name="routed_experts",
    )(tile_expert, tile_rows, n_tiles, src_tok, dst_row, x, w_gate, w_up, w_down, b_gate, b_up, b_down)


def _combine_ln_kernel(x_ref, y0_ref, y1_ref, y2_ref, y3_ref, gate_ref, g_ref, b_ref, o_ref, ob_ref, *, alpha):
    gates = gate_ref[...]
    ffn = None
    for kk, y_ref in enumerate((y0_ref, y1_ref, y2_ref, y3_ref)):
        term = gates[:, kk:kk + 1] * y_ref[...]
        ffn = term if ffn is None else ffn + term
    o = _layer_norm_rows(alpha * x_ref[...] + ffn, g_ref[...], b_ref[...])
    o_ref[...] = o
    ob_ref[...] = o.astype(BF16)


def combine_layer_norm(x, y, gates, g, b, layer, alpha, bt=128):
    s, d = x.shape
    bt = min(bt, s)
    nb = s // bt
    row = pl.BlockSpec((bt, d), lambda i: (i, 0))
    yk = lambda kk: pl.BlockSpec((bt, d), lambda i: (kk * nb + i, 0))
    par = pl.BlockSpec((None, 1, d), lambda i: (layer, 0, 0))
    return pl.pallas_call(
        functools.partial(_combine_ln_kernel, alpha=alpha),
        out_shape=(jax.ShapeDtypeStruct((s, d), F32), jax.ShapeDtypeStruct((s, d), BF16)),
        grid=(nb,),
        in_specs=[row, yk(0), yk(1), yk(2), yk(3), pl.BlockSpec((bt, TOP_K), lambda i: (i, 0)), par, par],
        out_specs=(row, row),
        compiler_params=_params(("parallel",), 2 * bt * d * (5 * 4 + 4 + 2)),
        name="combine_ln",
    )(x, y, y, y, y, gates, g, b)


def _routing_plan(idx, rank, counts, s, n_e):
    tm = MOE_TILE
    nt = s * TOP_K // tm + n_e
    counts = counts.reshape(n_e)
    tiles_e = (counts + tm - 1) // tm
    tile_end = jnp.cumsum(tiles_e)
    tile_start = tile_end - tiles_e
    n_tiles = tile_end[-1]
    tile_ids = jnp.arange(nt, dtype=I32)
    te = jnp.clip(jnp.searchsorted(tile_end, tile_ids, side="right"), 0, n_e - 1).astype(I32)
    last = te[jnp.maximum(n_tiles - 1, 0)]
    valid = tile_ids < n_tiles
    te = jnp.where(valid, te, last)
    rows = jnp.clip(counts[te] - (tile_ids - tile_start[te]) * tm, 0, tm)
    tile_rows = jnp.where(valid, rows, 0).astype(I32)
    pos = (tile_start * tm)[idx] + rank
    tok = jnp.arange(s, dtype=I32)[:, None]
    val = jnp.arange(TOP_K, dtype=I32)[None, :] * s + tok
    slots = jnp.zeros((nt * tm,), I32).at[pos.reshape(-1)].set(val.reshape(-1), unique_indices=True)
    dst_row = slots.reshape(nt, 1, tm)
    src_tok = (slots % s).reshape(nt, 1, tm)
    return te, tile_rows, n_tiles.reshape(1).astype(I32), src_tok, dst_row


def kernel(x, mem, positions, w_in, conv_w, a_log, dt_bias, dn_norm_w, lambda_q1, lambda_k1, lambda_q2,
           lambda_k2, da_norm_w, w_mem_kv, w_o, ln1_g, ln1_b, router_w, router_b, w_gate, b_gate, w_up, b_up,
           w_down, b_down, ln2_g, ln2_b):
    bsz, s, d = x.shape
    assert bsz == 1
    depth = w_in.shape[0]
    n_e = router_w.shape[-1]
    f = w_gate.shape[-1]
    dn_w = 3 * d // 8
    da_w = 3 * d // 8
    mem_w = d - dn_w - da_w
    dn_heads = dn_w // DN_HEAD_DIM
    alpha = (2 * depth) ** 0.25
    hb = DN_HEADS_PER_STEP
    groups = dn_heads // hb
    c_ = DN_CHUNK

    inv = 1.0 / (ROPE_THETA ** (jnp.arange(0, DA_HEAD_DIM, 2, dtype=F32) / DA_HEAD_DIM))
    ang = positions[0].astype(F32)[:, None] * inv
    ang = jnp.concatenate([ang, ang], -1)
    sign = jnp.concatenate([-jnp.ones((DA_HEAD_DIM // 2,), F32), jnp.ones((DA_HEAD_DIM // 2,), F32)])
    cos, sin_signed = jnp.cos(ang), jnp.sin(ang) * sign
    gate0 = 4 * dn_w
    post0 = gate0 + 2 * dn_heads
    w_post = w_in[:, :, post0:]
    w_gates = jnp.pad(w_in[:, :, gate0:post0], ((0, 0), (0, 0), (0, V7X_LANES - 2 * dn_heads)))
    lam_inits = [0.8 - 0.6 * math.exp(-0.3 * layer) for layer in range(depth)]
    vec3 = lambda p: p.reshape(depth, 1, -1)
    ln1_g3, ln1_b3, ln2_g3, ln2_b3 = vec3(ln1_g), vec3(ln1_b), vec3(ln2_g), vec3(ln2_b)
    lq1, lk1, lq2, lk2 = vec3(lambda_q1), vec3(lambda_k1), vec3(lambda_q2), vec3(lambda_k2)
    da_nw = vec3(da_norm_w)
    router_b3 = vec3(router_b)
    b_gate4 = b_gate.reshape(depth, n_e, 1, f)
    b_up4 = b_up.reshape(depth, n_e, 1, f)
    b_down4 = b_down.reshape(depth, n_e, 1, d)

    xf = x[0]
    xb = cast_bf16(xf)
    memb = cast_bf16(mem[0])
    for layer in range(depth):
        p1 = matmul([xb], w_in, layer, 4 * dn_w, name="proj_dn")
        p2 = matmul([xb], w_post, layer, w_post.shape[-1], name="proj_attn")
        gts = matmul([xb], w_gates, layer, V7X_LANES, bn=V7X_LANES, name="proj_gates")
        mem_kv = matmul([memb], w_mem_kv, layer, 2 * mem_w, name="proj_mem")

        ab = gts[:, :2 * dn_heads].reshape(s, 2, groups, hb)
        gates_col = ab.transpose(1, 2, 0, 3)
        gates_row = ab.reshape(s // c_, c_, 2, groups, hb).transpose(2, 3, 0, 4, 1)
        qkv = dn_prep(p1, conv_w, layer, dn_w)
        y_dn = deltanet(qkv, p1, gates_col, gates_row, a_log, dt_bias, dn_norm_w, layer, dn_w)

        qr, kr, vb = da_prep(p2, cos, sin_signed, da_w)
        lam_init = jnp.full((1,), lam_inits[layer], F32)
        y_da = diff_attention(qr, kr, vb, lq1, lk1, lq2, lk2, da_nw, lam_init, layer)
        y_mem = memory_attention(p2, mem_kv, 3 * da_w, mem_w)

        mixed = matmul([y_dn, y_da, y_mem], w_o, layer, d, name="proj_out")
        x1, _ = residual_layer_norm(xf, mixed, ln1_g3, ln1_b3, layer, alpha)

        idx, gates, rank, counts = router(x1, router_w, router_b3, layer)
        te, tile_rows, n_tiles, src_tok, dst_row = _routing_plan(idx, rank, counts, s, n_e)
        y = routed_experts(x1, te, tile_rows, n_tiles, src_tok, dst_row, w_gate, b_gate4, w_up, b_up4, w_down,
                           b_down4, layer)
        xf, xb = combine_layer_norm(x1, y, gates, ln2_g3, ln2_b3, layer, alpha)
    return xf[None]
```

```python
import functools
import math

import jax
import jax.numpy as jnp
from jax import lax
from jax.experimental import pallas as pl
from jax.experimental.pallas import tpu as pltpu

F32 = jnp.float32
BF16 = jnp.bfloat16
I32 = jnp.int32
HIGHEST = lax.Precision.HIGHEST

V7X_LANES = 128
V7X_SUBLANES = 8
V7X_VMEM_BYTES = 64 * 1024 * 1024
VMEM_COMPILER_MARGIN = 6 * 1024 * 1024

DN_HEAD_DIM = 128
CONV_K = 4
DA_HEAD_DIM = 128
DA_V_DIM = 2 * DA_HEAD_DIM
MEM_HEADS = 4
TOP_K = 4
ROPE_THETA = 10000.0
SWIGLU_LIMIT = 7.0
SWIGLU_ALPHA = 1.702
LN_EPS = 1e-5
RMS_EPS = 1e-6
L2_EPS = 1e-6
LOG2E = 1.4426950408889634

DN_CHUNK = 128
DN_HEADS_PER_STEP = 6
DN_INV_BASE = 16
MOE_TILE = 256
MOE_DMA_UNROLL = 8
MM_SHIFT_ROWS = 256
DA_ROW_CHUNK = 128


def _vmem_limit(nbytes):
    return int(min(nbytes + VMEM_COMPILER_MARGIN, V7X_VMEM_BYTES - 2 * 1024 * 1024))


def _params(semantics, vmem_bytes):
    return pltpu.CompilerParams(dimension_semantics=semantics, vmem_limit_bytes=_vmem_limit(vmem_bytes))


def _cast_kernel(x_ref, o_ref):
    o_ref[...] = x_ref[...].astype(o_ref.dtype)


def cast_bf16(x, bt=256):
    m, d = x.shape
    bt = min(bt, m)
    return pl.pallas_call(
        _cast_kernel,
        out_shape=jax.ShapeDtypeStruct((m, d), BF16),
        grid=(m // bt,),
        in_specs=[pl.BlockSpec((bt, d), lambda i: (i, 0))],
        out_specs=pl.BlockSpec((bt, d), lambda i: (i, 0)),
        compiler_params=_params(("parallel",), 2 * bt * d * 6),
        name="cast_bf16",
    )(x)


def _mm_kernel(*refs, k_splits, lane_shift):
    n_a = len(k_splits)
    a_refs = refs[:n_a]
    if lane_shift:
        w_ref, w_tail_ref, o_ref, wb_ref = refs[n_a:]
    else:
        w_ref, o_ref, wb_ref = refs[n_a:]

    @pl.when(pl.program_id(1) == 0)
    def _cast_weights():
        if not lane_shift:
            wb_ref[...] = w_ref[...].astype(BF16)
        else:
            k, bn = wb_ref.shape
            wide = bn + w_tail_ref.shape[1]

            def shift_rows(c, carry):
                rows = pl.ds(pl.multiple_of(c * MM_SHIFT_ROWS, MM_SHIFT_ROWS), MM_SHIFT_ROWS)
                both = jnp.concatenate([w_ref[rows, :], w_tail_ref[rows, :]], axis=1)
                wb_ref[rows, :] = pltpu.roll(both, wide - lane_shift, 1)[:, :bn].astype(BF16)
                return carry

            lax.fori_loop(0, k // MM_SHIFT_ROWS, shift_rows, 0)

    acc = None
    k0 = 0
    for a_ref, kk in zip(a_refs, k_splits):
        part = jnp.dot(a_ref[...], wb_ref[k0:k0 + kk, :], preferred_element_type=F32)
        acc = part if acc is None else acc + part
        k0 += kk
    o_ref[...] = acc.astype(o_ref.dtype)


def matmul(a_list, w, layer, ncols, *, col0=0, bm=1024, bn=512, out_dtype=F32, name="proj"):
    m = a_list[0].shape[0]
    k_splits = tuple(a.shape[1] for a in a_list)
    k = sum(k_splits)
    assert w.shape[1] == k and ncols % bn == 0
    bm = min(bm, m)
    assert m % bm == 0
    col_block0, lane_shift = divmod(col0, bn)
    assert lane_shift < V7X_LANES and k % MM_SHIFT_ROWS == 0
    in_specs = [pl.BlockSpec((bm, kk), lambda j, i: (i, 0)) for kk in k_splits]
    in_specs.append(pl.BlockSpec((None, k, bn), lambda j, i: (layer, 0, j + col_block0)))
    operands = list(a_list) + [w]
    if lane_shift:
        per = bn // V7X_LANES
        in_specs.append(pl.BlockSpec((None, k, V7X_LANES), lambda j, i: (layer, 0, (j + col_block0 + 1) * per)))
        operands.append(w)
    vmem = 2 * bm * k * 2 + 2 * k * (bn + V7X_LANES) * 4 + k * bn * 2 + 2 * bm * bn * 4
    return pl.pallas_call(
        functools.partial(_mm_kernel, k_splits=k_splits, lane_shift=lane_shift),
        out_shape=jax.ShapeDtypeStruct((m, ncols), out_dtype),
        grid=(ncols // bn, m // bm),
        in_specs=in_specs,
        out_specs=pl.BlockSpec((bm, bn), lambda j, i: (i, j)),
        scratch_shapes=[pltpu.VMEM((k, bn), BF16)],
        compiler_params=_params(("arbitrary", "arbitrary"), vmem),
        name=name,
    )(*operands)


def _layer_norm_rows(h, g, b):
    mu = jnp.mean(h, -1, keepdims=True)
    d = h - mu
    var = jnp.mean(d * d, -1, keepdims=True)
    return d * lax.rsqrt(var + LN_EPS) * g + b


def _ln_kernel(x_ref, y_ref, g_ref, b_ref, o_ref, ob_ref, *, alpha):
    o = _layer_norm_rows(alpha * x_ref[...] + y_ref[...], g_ref[...], b_ref[...])
    o_ref[...] = o
    ob_ref[...] = o.astype(BF16)


def residual_layer_norm(x, y, g, b, layer, alpha, bt=256):
    s, d = x.shape
    bt = min(bt, s)
    row = pl.BlockSpec((bt, d), lambda i: (i, 0))
    par = pl.BlockSpec((None, 1, d), lambda i: (layer, 0, 0))
    return pl.pallas_call(
        functools.partial(_ln_kernel, alpha=alpha),
        out_shape=(jax.ShapeDtypeStruct((s, d), F32), jax.ShapeDtypeStruct((s, d), BF16)),
        grid=(s // bt,),
        in_specs=[row, row, par, par],
        out_specs=(row, row),
        compiler_params=_params(("parallel",), 2 * bt * d * (4 + 4 + 4 + 2)),
        name="residual_ln",
    )(x, y, g, b)


def _dn_prep_kernel(x_ref, halo_ref, w_ref, o_ref, xs_ref, *, bt, n_q_blocks, n_qk_blocks, q_scale):
    i = pl.program_id(0)
    j = pl.program_id(1)
    halo = halo_ref[...]
    xs_ref[0:V7X_SUBLANES, :] = jnp.where(i == 0, jnp.zeros_like(halo), halo)
    xs_ref[V7X_SUBLANES:, :] = x_ref[...]
    w = w_ref[...]
    y = None
    for tap in range(CONV_K):
        start = V7X_SUBLANES - (CONV_K - 1) + tap
        term = xs_ref[start:start + bt, :] * w[tap:tap + 1, :]
        y = term if y is None else y + term
    y = y * jax.nn.sigmoid(y)
    bc = y.shape[1]
    scale = jnp.where(j < n_q_blocks, q_scale, 1.0).astype(F32)
    for c in range(bc // DN_HEAD_DIM):
        sl = slice(c * DN_HEAD_DIM, (c + 1) * DN_HEAD_DIM)
        yh = y[:, sl]
        nrm = yh * (lax.rsqrt(jnp.sum(yh * yh, -1, keepdims=True) + L2_EPS) * scale)
        o_ref[:, sl] = jnp.where(j < n_qk_blocks, nrm, yh).astype(BF16)


def dn_prep(p1, conv_w, layer, dn_width, bt=512, bc=512):
    s = p1.shape[0]
    ncols = 3 * dn_width
    bt = min(bt, s)
    assert dn_width % bc == 0 and s % bt == 0
    hb = bt // V7X_SUBLANES
    return pl.pallas_call(
        functools.partial(_dn_prep_kernel, bt=bt, n_q_blocks=dn_width // bc, n_qk_blocks=2 * dn_width // bc,
                          q_scale=DN_HEAD_DIM ** -0.5),
        out_shape=jax.ShapeDtypeStruct((s, ncols), BF16),
        grid=(s // bt, ncols // bc),
        in_specs=[
            pl.BlockSpec((bt, bc), lambda i, j: (i, j)),
            pl.BlockSpec((V7X_SUBLANES, bc), lambda i, j: (jnp.maximum(i * hb - 1, 0), j)),
            pl.BlockSpec((None, CONV_K, bc), lambda i, j: (layer, 0, j)),
        ],
        out_specs=pl.BlockSpec((bt, bc), lambda i, j: (i, j)),
        scratch_shapes=[pltpu.VMEM((bt + V7X_SUBLANES, bc), F32)],
        compiler_params=_params(("parallel", "parallel"), 4 * bt * bc * 4),
        name="dn_prep",
    )(p1, p1, conv_w)


def _softplus(x):
    return jnp.maximum(x, 0.0) + jnp.log1p(jnp.exp(-jnp.abs(x)))


def _dot32(a, b):
    return jnp.dot(a, b, preferred_element_type=F32)


def _unit_lower_inverses(a_list, row, col):
    c = a_list[0].shape[0]
    hs = range(len(a_list))
    eye = (row == col).astype(F32)
    dist = row ^ col
    n = [jnp.where(dist < DN_INV_BASE, -a, 0.0) for a in a_list]
    t = [eye + n[h] for h in hs]
    p = [n[h].astype(BF16) for h in hs]
    for _ in range(int(math.log2(DN_INV_BASE)) - 1):
        p = [_dot32(p[h], p[h]).astype(BF16) for h in hs]
        t = [t[h] + _dot32(t[h].astype(BF16), p[h]) for h in hs]
    b = DN_INV_BASE
    while b < c:
        pair = (dist >= b) & (dist < 2 * b)
        mb = [jnp.where(pair, a, 0.0).astype(BF16) for a in a_list]
        tb = [t[h].astype(BF16) for h in hs]
        tm = [_dot32(tb[h], mb[h]).astype(BF16) for h in hs]
        t = [t[h] - _dot32(tm[h], tb[h]) for h in hs]
        b *= 2
    return t


def _dn_kernel(q_ref, k_ref, v_ref, z_ref, ac_ref, bc_ref, ar_ref, br_ref, alc_ref, dtc_ref, alr_ref, dtr_ref,
               nw_ref, o_ref, state_ref, *, hb, chunk, n_chunks):
    c_ = chunk
    d_ = DN_HEAD_DIM

    @pl.when(pl.program_id(1) == 0)
    def _reset():
        state_ref[...] = jnp.zeros_like(state_ref)

    row = lax.broadcasted_iota(I32, (c_, c_), 0)
    col = lax.broadcasted_iota(I32, (c_, c_), 1)
    incl = row >= col
    strict = row > col
    ltri = incl.astype(F32)
    neg_decay_c = -jnp.exp(alc_ref[...])
    neg_decay_r = -jnp.exp(alr_ref[...])
    nw = nw_ref[...]
    contract_last = (((1,), (1,)), ((), ()))
    contract_first = (((0,), (0,)), ((), ()))

    def chunk_body(ci, carry):
        r0 = pl.multiple_of(ci * c_, c_)
        rows = pl.ds(r0, c_)
        g_c = neg_decay_c * _softplus(ac_ref[rows, :] + dtc_ref[...])
        beta_c = jax.nn.sigmoid(bc_ref[rows, :])
        cum_c = jnp.dot(ltri, g_c, precision=HIGHEST, preferred_element_type=F32)
        g_r = neg_decay_r * _softplus(ar_ref[ci] + dtr_ref[...])
        cum_r = lax.dot_general(g_r, ltri, contract_last, precision=HIGHEST, preferred_element_type=F32)
        hs = range(hb)
        cols = [slice(hh * d_, (hh + 1) * d_) for hh in hs]
        gc = [cum_c[:, hh:hh + 1] for hh in hs]
        gr = [cum_r[hh:hh + 1, :] for hh in hs]
        bcol = [beta_c[:, hh:hh + 1] for hh in hs]
        qh = [q_ref[rows, cols[hh]] for hh in hs]
        kh = [k_ref[rows, cols[hh]] for hh in hs]
        vh = [v_ref[rows, cols[hh]] for hh in hs]
        kk = [lax.dot_general(kh[hh], kh[hh], contract_last, preferred_element_type=F32) for hh in hs]
        qk = [lax.dot_general(qh[hh], kh[hh], contract_last, preferred_element_type=F32) for hh in hs]
        decay = [jnp.exp(jnp.where(incl, gc[hh] - gr[hh], -jnp.inf)) for hh in hs]
        a = [jnp.where(strict, bcol[hh] * kk[hh] * decay[hh], 0.0) for hh in hs]
        t = _unit_lower_inverses(a, row, col)
        e_g = [jnp.exp(gc[hh]) for hh in hs]
        kf = [kh[hh].astype(F32) for hh in hs]
        rhs = [jnp.concatenate([vh[hh].astype(F32) * bcol[hh], kf[hh] * (bcol[hh] * e_g[hh])], axis=1).astype(BF16)
               for hh in hs]
        uw = [_dot32(t[hh].astype(BF16), rhs[hh]) for hh in hs]
        state = [state_ref[hh] for hh in hs]
        wq = [jnp.concatenate([uw[hh][:, d_:].astype(BF16), (qh[hh].astype(F32) * e_g[hh]).astype(BF16)], axis=0)
              for hh in hs]
        ws = [_dot32(wq[hh], state[hh].astype(BF16)) for hh in hs]
        v_new = [(uw[hh][:, :d_] - ws[hh][:c_]).astype(BF16) for hh in hs]
        o = [ws[hh][c_:] + _dot32((qk[hh] * decay[hh]).astype(BF16), v_new[hh]) for hh in hs]
        g_last = [gc[hh][c_ - 1:c_, :] for hh in hs]
        k_dec = [(kf[hh] * jnp.exp(g_last[hh] - gc[hh])).astype(BF16) for hh in hs]
        for hh in hs:
            state_ref[hh] = state[hh] * jnp.exp(g_last[hh]) + lax.dot_general(
                k_dec[hh], v_new[hh], contract_first, preferred_element_type=F32)
        for hh in hs:
            zf = z_ref[rows, cols[hh]]
            on = o[hh] * lax.rsqrt(jnp.mean(o[hh] * o[hh], -1, keepdims=True) + RMS_EPS) * nw
            o_ref[rows, cols[hh]] = (on * (zf * jax.nn.sigmoid(zf))).astype(BF16)
        return carry

    lax.fori_loop(0, n_chunks, chunk_body, 0)


def deltanet(qkv, p1, gates_col, gates_row, a_log, dt_bias, norm_w, layer, dn_width, bs=1024):
    s = qkv.shape[0]
    hb = DN_HEADS_PER_STEP
    c_ = DN_CHUNK
    heads = dn_width // DN_HEAD_DIM
    assert heads % hb == 0
    groups = heads // hb
    bs = min(bs, s)
    assert s % bs == 0 and bs % c_ == 0
    wblk = hb * DN_HEAD_DIM
    ncb = dn_width // wblk
    n_chunks = bs // c_
    a_log_c = a_log.reshape(-1, groups, 1, hb)
    a_log_r = a_log.reshape(-1, groups, hb, 1)
    dt_c = dt_bias.reshape(-1, groups, 1, hb)
    dt_r = dt_bias.reshape(-1, groups, hb, 1)
    nw = norm_w.reshape(-1, 1, DN_HEAD_DIM)

    def seg(which):
        return pl.BlockSpec((bs, wblk), lambda g, i: (i, which * ncb + g))

    col_spec = lambda w: pl.BlockSpec((None, None, bs, hb), lambda g, i: (w, g, i, 0))
    row_spec = lambda w: pl.BlockSpec((None, None, n_chunks, hb, c_), lambda g, i: (w, g, i, 0, 0))
    pc_spec = pl.BlockSpec((None, None, 1, hb), lambda g, i: (layer, g, 0, 0))
    pr_spec = pl.BlockSpec((None, None, hb, 1), lambda g, i: (layer, g, 0, 0))
    vmem = 2 * bs * wblk * (2 * 3 + 4 + 2) + hb * DN_HEAD_DIM * DN_HEAD_DIM * 4 + 16 * bs * V7X_LANES * 4
    return pl.pallas_call(
        functools.partial(_dn_kernel, hb=hb, chunk=c_, n_chunks=n_chunks),
        out_shape=jax.ShapeDtypeStruct((s, dn_width), BF16),
        grid=(groups, s // bs),
        in_specs=[seg(0), seg(1), seg(2), seg(3),
                  col_spec(0), col_spec(1), row_spec(0), row_spec(1),
                  pc_spec, pc_spec, pr_spec, pr_spec,
                  pl.BlockSpec((None, 1, DN_HEAD_DIM), lambda g, i: (layer, 0, 0))],
        out_specs=pl.BlockSpec((bs, wblk), lambda g, i: (i, g)),
        scratch_shapes=[pltpu.VMEM((hb, DN_HEAD_DIM, DN_HEAD_DIM), F32)],
        compiler_params=_params(("parallel", "arbitrary"), vmem),
        name="deltanet",
    )(qkv, qkv, qkv, p1, gates_col, gates_col, gates_row, gates_row, a_log_c, dt_c, a_log_r, dt_r, nw)


def _da_prep_kernel(q_ref, k_ref, v_ref, cos_ref, sin_ref, qo_ref, ko_ref, vo_ref, *, q_scale):
    c = cos_ref[...]
    s = sin_ref[...]
    half = DA_HEAD_DIM // 2
    for j in range(q_ref.shape[1] // DA_HEAD_DIM):
        sl = slice(j * DA_HEAD_DIM, (j + 1) * DA_HEAD_DIM)
        xq = q_ref[:, sl]
        xk = k_ref[:, sl]
        qo_ref[:, sl] = ((xq * c + pltpu.roll(xq, half, 1) * s) * q_scale).astype(BF16)
        ko_ref[:, sl] = (xk * c + pltpu.roll(xk, half, 1) * s).astype(BF16)
    vo_ref[...] = v_ref[...].astype(BF16)


def da_prep(p2, cos, sin_signed, da_width, bt=256):
    s = p2.shape[0]
    bt = min(bt, s)
    seg = lambda w: pl.BlockSpec((bt, da_width), lambda i: (i, w))
    tab = pl.BlockSpec((bt, DA_HEAD_DIM), lambda i: (i, 0))
    out = jax.ShapeDtypeStruct((s, da_width), BF16)
    return pl.pallas_call(
        functools.partial(_da_prep_kernel, q_scale=DA_HEAD_DIM ** -0.5 * LOG2E),
        out_shape=(out, out, out),
        grid=(s // bt,),
        in_specs=[seg(0), seg(1), seg(2), tab, tab],
        out_specs=(seg(0), seg(0), seg(0)),
        compiler_params=_params(("parallel",), 2 * 3 * bt * da_width * 6),
        name="da_prep",
    )(p2, p2, p2, cos, sin_signed)


def _da_kernel(lam_init_ref, q_ref, k_ref, v_ref, lq1_ref, lk1_ref, lq2_ref, lk2_ref, nw_ref, o_ref,
               acc_ref, m_ref, l_ref, *, blk):
    iq = pl.program_id(1)
    d_ = DA_HEAD_DIM
    rc = DA_ROW_CHUNK
    contract_last = (((1,), (1,)), ((), ()))
    m_ref[...] = jnp.full_like(m_ref, -jnp.inf)
    l_ref[...] = jnp.zeros_like(l_ref)
    acc_ref[...] = jnp.zeros_like(acc_ref)
    chains = [(c, sub) for c in range(blk // rc) for sub in range(2)]
    rsl = lambda c: slice(c * rc, (c + 1) * rc)
    lanes = lambda x, width: jnp.concatenate([x] * (width // V7X_LANES), axis=1)

    def kv_step(j, masked):
        r0 = pl.multiple_of(j * blk, blk)
        kb = k_ref[pl.ds(r0, blk), :]
        vb = v_ref[pl.ds(r0, blk), :]
        s = [lax.dot_general(q_ref[rsl(c), sub * d_:(sub + 1) * d_], kb[:, sub * d_:(sub + 1) * d_], contract_last,
                             preferred_element_type=F32) for c, sub in chains]
        if masked:
            row = lax.broadcasted_iota(I32, (rc, blk), 0)
            col = lax.broadcasted_iota(I32, (rc, blk), 1)
            s = [jnp.where(col <= row + c * rc, s[n], -jnp.inf) for n, (c, sub) in enumerate(chains)]
        m_old = [m_ref[sub, rsl(c), :] for c, sub in chains]
        m_new = [jnp.maximum(m_old[n], jnp.max(s[n], -1, keepdims=True)) for n in range(len(chains))]
        alpha = [jnp.exp2(m_old[n] - m_new[n]) for n in range(len(chains))]
        p = [jnp.exp2(s[n] - lanes(m_new[n], blk)) for n in range(len(chains))]
        for n, (c, sub) in enumerate(chains):
            m_ref[sub, rsl(c), :] = m_new[n]
            l_ref[sub, rsl(c), :] = alpha[n] * l_ref[sub, rsl(c), :] + jnp.sum(p[n], -1, keepdims=True)
        pv = [jnp.dot(p[n].astype(BF16), vb, preferred_element_type=F32) for n in range(len(chains))]
        for n, (c, sub) in enumerate(chains):
            acc_ref[sub, rsl(c), :] = lanes(alpha[n], DA_V_DIM) * acc_ref[sub, rsl(c), :] + pv[n]

    def body(j, carry):
        kv_step(j, False)
        return carry

    lax.fori_loop(0, iq, body, 0)
    kv_step(iq, True)

    lam_init = lam_init_ref[0]
    lam = (jnp.exp(jnp.sum(lq1_ref[...] * lk1_ref[...], keepdims=True))
           - jnp.exp(jnp.sum(lq2_ref[...] * lk2_ref[...], keepdims=True)) + lam_init)
    o = acc_ref[0] / lanes(l_ref[0], DA_V_DIM) - lam * (acc_ref[1] / lanes(l_ref[1], DA_V_DIM))
    on = o * lax.rsqrt(jnp.mean(o * o, -1, keepdims=True) + RMS_EPS) * nw_ref[...]
    o_ref[...] = (on * (1.0 - lam_init)).astype(BF16)


def diff_attention(qr, kr, vb, lq1, lk1, lq2, lk2, norm_w, lam_init, layer, blk=512):
    s, width = qr.shape
    heads = width // DA_V_DIM
    blk = min(blk, s)
    assert s % blk == 0
    vec = pl.BlockSpec((None, 1, DA_HEAD_DIM), lambda h, i, *_: (layer, 0, 0))
    full = pl.BlockSpec((s, DA_V_DIM), lambda h, i, *_: (0, h))
    tile = pl.BlockSpec((blk, DA_V_DIM), lambda h, i, *_: (i, h))
    grid_spec = pltpu.PrefetchScalarGridSpec(
        num_scalar_prefetch=1,
        grid=(heads, s // blk),
        in_specs=[tile, full, full, vec, vec, vec, vec,
                  pl.BlockSpec((None, 1, DA_V_DIM), lambda h, i, *_: (layer, 0, 0))],
        out_specs=tile,
        scratch_shapes=[pltpu.VMEM((2, blk, DA_V_DIM), F32), pltpu.VMEM((2, blk, V7X_LANES), F32),
                        pltpu.VMEM((2, blk, V7X_LANES), F32)],
    )
    vmem = 4 * s * DA_V_DIM * 2 + 4 * blk * DA_V_DIM * 2 + 2 * blk * DA_V_DIM * 4 + 8 * blk * blk * 4
    return pl.pallas_call(
        functools.partial(_da_kernel, blk=blk),
        out_shape=jax.ShapeDtypeStruct((s, width), BF16),
        grid_spec=grid_spec,
        compiler_params=_params(("parallel", "arbitrary"), vmem),
        name="diff_attention",
    )(lam_init, qr, kr, vb, lq1, lk1, lq2, lk2, norm_w)


def _mem_attn_kernel(q_ref, k_ref, v_ref, o_ref, *, scale):
    q = (q_ref[...] * scale).astype(BF16)
    k = k_ref[...].astype(BF16)
    sc = lax.dot_general(q, k, (((1,), (1,)), ((), ())), preferred_element_type=F32)
    sc = sc - jnp.max(sc, -1, keepdims=True)
    p = jnp.exp(sc)
    denom = jnp.sum(p, -1, keepdims=True)
    o = jnp.dot(p.astype(BF16), v_ref[...].astype(BF16), preferred_element_type=F32) / denom
    o_ref[...] = o.astype(BF16)


def memory_attention(p2, mem_kv, q_col0, mem_width, bt=512):
    s = p2.shape[0]
    n_mem = mem_kv.shape[0]
    hd = mem_width // MEM_HEADS
    bt = min(bt, s)
    assert q_col0 % hd == 0
    qb0 = q_col0 // hd
    return pl.pallas_call(
        functools.partial(_mem_attn_kernel, scale=hd ** -0.5),
        out_shape=jax.ShapeDtypeStruct((s, mem_width), BF16),
        grid=(s // bt, MEM_HEADS),
        in_specs=[pl.BlockSpec((bt, hd), lambda i, h: (i, qb0 + h)),
                  pl.BlockSpec((n_mem, hd), lambda i, h: (0, h)),
                  pl.BlockSpec((n_mem, hd), lambda i, h: (0, MEM_HEADS + h))],
        out_specs=pl.BlockSpec((bt, hd), lambda i, h: (i, h)),
        compiler_params=_params(("parallel", "parallel"), 4 * bt * hd * 4 + 4 * n_mem * hd * 4 + 4 * bt * n_mem * 4),
        name="memory_attention",
    )(p2, mem_kv, mem_kv)


def _router_kernel(x_ref, w_ref, b_ref, idx_ref, gate_ref, rank_ref, cnt_ref, carry_ref):
    i = pl.program_id(0)

    @pl.when(i == 0)
    def _reset():
        carry_ref[...] = jnp.zeros_like(carry_ref)

    logits = jnp.dot(x_ref[...], w_ref[...], precision=HIGHEST, preferred_element_type=F32) + b_ref[...]
    bt, n_e = logits.shape
    lane = lax.broadcasted_iota(I32, (bt, n_e), 1)
    work = logits
    vals, sels = [], []
    for kk in range(TOP_K):
        mx = jnp.max(work, -1, keepdims=True)
        idx = jnp.min(jnp.where(work == mx, lane, n_e), -1, keepdims=True)
        sel = lane == idx
        vals.append(mx)
        sels.append(sel)
        idx_ref[:, kk:kk + 1] = idx
        work = jnp.where(sel, -jnp.inf, work)
    exps = [jnp.exp(v - vals[0]) for v in vals]
    denom = exps[0]
    for e in exps[1:]:
        denom = denom + e
    for kk in range(TOP_K):
        gate_ref[:, kk:kk + 1] = exps[kk] / denom
    mask = sels[0]
    for sel in sels[1:]:
        mask = mask | sel
    maskf = mask.astype(F32)
    row = lax.broadcasted_iota(I32, (bt, bt), 0)
    col = lax.broadcasted_iota(I32, (bt, bt), 1)
    before = (row > col).astype(BF16)
    rank = carry_ref[...] + jnp.dot(before, maskf.astype(BF16), preferred_element_type=F32)
    for kk in range(TOP_K):
        rank_ref[:, kk:kk + 1] = jnp.sum(jnp.where(sels[kk], rank, 0.0), -1, keepdims=True).astype(I32)
    carry_ref[...] = carry_ref[...] + jnp.sum(maskf, 0, keepdims=True)
    cnt_ref[...] = carry_ref[...].astype(I32)


def router(x, router_w, router_b, layer, bt=256):
    s, d = x.shape
    n_e = router_w.shape[-1]
    bt = min(bt, s)
    small = lambda dt: jax.ShapeDtypeStruct((s, TOP_K), dt)
    tok = pl.BlockSpec((bt, TOP_K), lambda i: (i, 0))
    return pl.pallas_call(
        _router_kernel,
        out_shape=(small(I32), small(F32), small(I32), jax.ShapeDtypeStruct((1, n_e), I32)),
        grid=(s // bt,),
        in_specs=[pl.BlockSpec((bt, d), lambda i: (i, 0)),
                  pl.BlockSpec((None, d, n_e), lambda i: (layer, 0, 0)),
                  pl.BlockSpec((None, 1, n_e), lambda i: (layer, 0, 0))],
        out_specs=(tok, tok, tok, pl.BlockSpec((1, n_e), lambda i: (0, 0))),
        scratch_shapes=[pltpu.VMEM((1, n_e), F32)],
        compiler_params=_params(("arbitrary",), 2 * bt * d * 4 + 2 * d * V7X_LANES * 4),
        name="router",
    )(x, router_w, router_b)


def _row_copy(src, src_row, dst, dst_row, sem):
    return pltpu.make_async_copy(src.at[pl.ds(src_row, 1)], dst.at[pl.ds(dst_row, 1)], sem)


def _moe_kernel(te_ref, nt_ref, src_ref, src_next_ref, dst_ref, x_hbm, wg_ref, wu_ref, wd_ref, bg_ref, bu_ref,
                bd_ref, y_hbm, xbuf, ybuf, wgb, wub, wdb, gsem, ssem, *, tm):
    i = pl.program_id(0)
    nt = nt_ref[0]
    slot = lax.rem(i, 2)
    other = 1 - slot

    def start_gather(idx_ref, s_):
        for r in range(tm):
            _row_copy(x_hbm, idx_ref[0, r], xbuf.at[s_], r, gsem.at[s_]).start()

    def start_scatter(s_):
        for r in range(tm):
            _row_copy(ybuf.at[s_], r, y_hbm, dst_ref[0, r], ssem.at[s_]).start()

    def wait_gather(s_):
        pltpu.make_async_copy(x_hbm.at[pl.ds(0, tm)], xbuf.at[s_], gsem.at[s_]).wait()

    def wait_scatter(s_):
        pltpu.make_async_copy(ybuf.at[s_], y_hbm.at[pl.ds(0, tm)], ssem.at[s_]).wait()

    @pl.when(i == 0)
    def _init_spare_rows():
        ybuf[other] = jnp.zeros((tm, ybuf.shape[2]), F32)
        spare = pltpu.make_async_copy(ybuf.at[other], y_hbm.at[pl.ds(y_hbm.shape[0] - tm, tm)], ssem.at[other])
        spare.start()
        spare.wait()

    @pl.when((i == 0) & (nt > 0))
    def _first_gather():
        start_gather(src_ref, slot)

    @pl.when(i < nt)
    def _tile():
        wait_gather(slot)

        @pl.when(i + 1 < nt)
        def _next_gather():
            start_gather(src_next_ref, other)

        prev = te_ref[jnp.maximum(i - 1, 0)]

        @pl.when((i == 0) | (te_ref[i] != prev))
        def _cast_weights():
            wgb[...] = wg_ref[...].astype(BF16)
            wub[...] = wu_ref[...].astype(BF16)
            wdb[...] = wd_ref[...].astype(BF16)

        xb = xbuf[slot].astype(BF16)
        gate = jnp.dot(xb, wgb[...], preferred_element_type=F32) + bg_ref[...]
        up = jnp.dot(xb, wub[...], preferred_element_type=F32) + bu_ref[...]
        gate = jnp.minimum(gate, SWIGLU_LIMIT)
        up = jnp.clip(up, -SWIGLU_LIMIT, SWIGLU_LIMIT)
        act = (up + 1.0) * gate * jax.nn.sigmoid(SWIGLU_ALPHA * gate)
        ybuf[slot] = jnp.dot(act.astype(BF16), wdb[...], preferred_element_type=F32) + bd_ref[...]

        @pl.when(i > 0)
        def _drain_previous():
            wait_scatter(other)

        start_scatter(slot)

        @pl.when(i == nt - 1)
        def _drain_last():
            wait_scatter(slot)


def routed_experts(x, tile_expert, n_tiles, src_tok, dst_row, w_gate, b_gate, w_up, b_up, w_down, b_down, layer):
    s, d = x.shape
    f = w_gate.shape[-1]
    tm = MOE_TILE
    nt = tile_expert.shape[0]
    idx_spec = pl.BlockSpec((None, 1, tm), lambda i, *_: (i, 0, 0), memory_space=pltpu.SMEM)
    idx_next = pl.BlockSpec((None, 1, tm), lambda i, *_: (jnp.minimum(i + 1, nt - 1), 0, 0),
                            memory_space=pltpu.SMEM)
    wspec = lambda a, b: pl.BlockSpec((None, None, a, b), lambda i, te, n: (layer, te[i], 0, 0))
    grid_spec = pltpu.PrefetchScalarGridSpec(
        num_scalar_prefetch=2,
        grid=(nt,),
        in_specs=[idx_spec, idx_next, idx_spec, pl.BlockSpec(memory_space=pl.ANY),
                  wspec(d, f), wspec(d, f), wspec(f, d), wspec(1, f), wspec(1, f), wspec(1, d)],
        out_specs=pl.BlockSpec(memory_space=pl.ANY),
        scratch_shapes=[pltpu.VMEM((2, tm, d), F32), pltpu.VMEM((2, tm, d), F32),
                        pltpu.VMEM((d, f), BF16), pltpu.VMEM((d, f), BF16), pltpu.VMEM((f, d), BF16),
                        pltpu.SemaphoreType.DMA((2,)), pltpu.SemaphoreType.DMA((2,))],
    )
    vmem = 2 * 3 * d * f * 4 + 3 * d * f * 2 + 4 * tm * d * 4
    return pl.pallas_call(
        functools.partial(_moe_kernel, tm=tm),
        out_shape=jax.ShapeDtypeStruct((TOP_K * s + tm, d), F32),
        grid_spec=grid_spec,
        compiler_params=_params(("arbitrary",), vmem),
        name="routed_experts",
    )(tile_expert, n_tiles, src_tok, src_tok, dst_row, x, w_gate, w_up, w_down, b_gate, b_up, b_down)


def _combine_ln_kernel(x_ref, y0_ref, y1_ref, y2_ref, y3_ref, gate_ref, g_ref, b_ref, o_ref, ob_ref, *, alpha):
    gates = gate_ref[...]
    ffn = None
    for kk, y_ref in enumerate((y0_ref, y1_ref, y2_ref, y3_ref)):
        term = gates[:, kk:kk + 1] * y_ref[...]
        ffn = term if ffn is None else ffn + term
    o = _layer_norm_rows(alpha * x_ref[...] + ffn, g_ref[...], b_ref[...])
    o_ref[...] = o
    ob_ref[...] = o.astype(BF16)


def combine_layer_norm(x, y, gates, g, b, layer, alpha, bt=128):
    s, d = x.shape
    bt = min(bt, s)
    nb = s // bt
    row = pl.BlockSpec((bt, d), lambda i: (i, 0))
    yk = lambda kk: pl.BlockSpec((bt, d), lambda i: (kk * nb + i, 0))
    par = pl.BlockSpec((None, 1, d), lambda i: (layer, 0, 0))
    return pl.pallas_call(
        functools.partial(_combine_ln_kernel, alpha=alpha),
        out_shape=(jax.ShapeDtypeStruct((s, d), F32), jax.ShapeDtypeStruct((s, d), BF16)),
        grid=(nb,),
        in_specs=[row, yk(0), yk(1), yk(2), yk(3), pl.BlockSpec((bt, TOP_K), lambda i: (i, 0)), par, par],
        out_specs=(row, row),
        compiler_params=_params(("parallel",), 2 * bt * d * (5 * 4 + 4 + 2)),
        name="combine_ln",
    )(x, y, y, y, y, gates, g, b)


def _routing_plan(idx, rank, counts, s, n_e):
    tm = MOE_TILE
    nt = s * TOP_K // tm + n_e
    counts = counts.reshape(n_e)
    tiles_e = (counts + tm - 1) // tm
    tile_end = jnp.cumsum(tiles_e)
    tile_start = tile_end - tiles_e
    n_tiles = tile_end[-1]
    tile_ids = jnp.arange(nt, dtype=I32)
    te = jnp.sum((tile_end[None, :] <= jnp.minimum(tile_ids, n_tiles - 1)[:, None]).astype(I32), axis=1)
    te = jnp.clip(te, 0, n_e - 1).astype(I32)
    pos = (tile_start * tm)[idx] + rank
    tok = jnp.arange(s, dtype=I32)[:, None]
    val = jnp.arange(TOP_K, dtype=I32)[None, :] * s + tok
    spare = jnp.tile(TOP_K * s + jnp.arange(tm, dtype=I32), nt)
    slots = spare.at[pos.reshape(-1)].set(val.reshape(-1), unique_indices=True)
    dst_row = slots.reshape(nt, 1, tm)
    src_tok = (slots % s).reshape(nt, 1, tm)
    return te, n_tiles.reshape(1).astype(I32), src_tok, dst_row


def kernel(x, mem, positions, w_in, conv_w, a_log, dt_bias, dn_norm_w, lambda_q1, lambda_k1, lambda_q2,
           lambda_k2, da_norm_w, w_mem_kv, w_o, ln1_g, ln1_b, router_w, router_b, w_gate, b_gate, w_up, b_up,
           w_down, b_down, ln2_g, ln2_b):
    bsz, s, d = x.shape
    assert bsz == 1
    depth = w_in.shape[0]
    n_e = router_w.shape[-1]
    f = w_gate.shape[-1]
    dn_w = 3 * d // 8
    da_w = 3 * d // 8
    mem_w = d - dn_w - da_w
    dn_heads = dn_w // DN_HEAD_DIM
    alpha = (2 * depth) ** 0.25
    hb = DN_HEADS_PER_STEP
    groups = dn_heads // hb
    c_ = DN_CHUNK

    inv = 1.0 / (ROPE_THETA ** (jnp.arange(0, DA_HEAD_DIM, 2, dtype=F32) / DA_HEAD_DIM))
    ang = positions[0].astype(F32)[:, None] * inv
    ang = jnp.concatenate([ang, ang], -1)
    sign = jnp.concatenate([-jnp.ones((DA_HEAD_DIM // 2,), F32), jnp.ones((DA_HEAD_DIM // 2,), F32)])
    cos, sin_signed = jnp.cos(ang), jnp.sin(ang) * sign
    gate0 = 4 * dn_w
    post0 = gate0 + 2 * dn_heads
    lam_inits = [0.8 - 0.6 * math.exp(-0.3 * layer) for layer in range(depth)]
    vec3 = lambda p: p.reshape(depth, 1, -1)
    ln1_g3, ln1_b3, ln2_g3, ln2_b3 = vec3(ln1_g), vec3(ln1_b), vec3(ln2_g), vec3(ln2_b)
    lq1, lk1, lq2, lk2 = vec3(lambda_q1), vec3(lambda_k1), vec3(lambda_q2), vec3(lambda_k2)
    da_nw = vec3(da_norm_w)
    router_b3 = vec3(router_b)
    b_gate4 = b_gate.reshape(depth, n_e, 1, f)
    b_up4 = b_up.reshape(depth, n_e, 1, f)
    b_down4 = b_down.reshape(depth, n_e, 1, d)

    xf = x[0]
    xb = cast_bf16(xf)
    memb = cast_bf16(mem[0])
    for layer in range(depth):
        p1 = matmul([xb], w_in, layer, 4 * dn_w, name="proj_dn")
        p2 = matmul([xb], w_in, layer, 3 * da_w + mem_w, col0=post0, name="proj_attn")
        gts = matmul([xb], w_in, layer, V7X_LANES, col0=gate0, bn=V7X_LANES, name="proj_gates")
        mem_kv = matmul([memb], w_mem_kv, layer, 2 * mem_w, name="proj_mem")

        ab = gts[:, :2 * dn_heads].reshape(s, 2, groups, hb)
        gates_col = ab.transpose(1, 2, 0, 3)
        gates_row = ab.reshape(s // c_, c_, 2, groups, hb).transpose(2, 3, 0, 4, 1)
        qkv = dn_prep(p1, conv_w, layer, dn_w)
        y_dn = deltanet(qkv, p1, gates_col, gates_row, a_log, dt_bias, dn_norm_w, layer, dn_w)

        qr, kr, vb = da_prep(p2, cos, sin_signed, da_w)
        lam_init = jnp.full((1,), lam_inits[layer], F32)
        y_da = diff_attention(qr, kr, vb, lq1, lk1, lq2, lk2, da_nw, lam_init, layer)
        y_mem = memory_attention(p2, mem_kv, 3 * da_w, mem_w)

        mixed = matmul([y_dn, y_da, y_mem], w_o, layer, d, name="proj_out")
        x1, _ = residual_layer_norm(xf, mixed, ln1_g3, ln1_b3, layer, alpha)

        idx, gates, rank, counts = router(x1, router_w, router_b3, layer)
        te, n_tiles, src_tok, dst_row = _routing_plan(idx, rank, counts, s, n_e)
        y = routed_experts(x1, te, n_tiles, src_tok, dst_row, w_gate, b_gate4, w_up, b_up4, w_down, b_down4, layer)
        xf, xb = combine_layer_norm(x1, y, gates, ln2_g3, ln2_b3, layer, alpha)
    return xf[None]
```

```python
import functools
import math

import jax
import jax.numpy as jnp
from jax import lax
from jax.experimental import pallas as pl
from jax.experimental.pallas import tpu as pltpu

F32 = jnp.float32
BF16 = jnp.bfloat16
I32 = jnp.int32
U32 = jnp.uint32
HIGHEST = lax.Precision.HIGHEST
HIGH_HALF = 0xFFFF0000

V7X_LANES = 128
V7X_SUBLANES = 8
V7X_VMEM_BYTES = 64 * 1024 * 1024
VMEM_COMPILER_MARGIN = 6 * 1024 * 1024

DN_HEAD_DIM = 128
CONV_K = 4
DA_HEAD_DIM = 128
DA_V_DIM = 2 * DA_HEAD_DIM
MEM_HEADS = 4
TOP_K = 4
ROPE_THETA = 10000.0
SWIGLU_LIMIT = 7.0
SWIGLU_ALPHA = 1.702
LN_EPS = 1e-5
RMS_EPS = 1e-6
L2_EPS = 1e-6
LOG2E = 1.4426950408889634

DN_CHUNK = 128
DN_HEADS_PER_STEP = 6
DN_INV_BASE = 16
MOE_TILE = 256
MOE_DMA_UNROLL = 8
MM_CAST_COLS = 512
DA_ROW_CHUNK = 128


def _vmem_limit(nbytes):
    return int(min(nbytes + VMEM_COMPILER_MARGIN, V7X_VMEM_BYTES - 2 * 1024 * 1024))


def _params(semantics, vmem_bytes):
    return pltpu.CompilerParams(dimension_semantics=semantics, vmem_limit_bytes=_vmem_limit(vmem_bytes))


def _cast_kernel(x_ref, o_ref):
    o_ref[...] = x_ref[...].astype(o_ref.dtype)


def cast_bf16(x, bt=256):
    m, d = x.shape
    bt = min(bt, m)
    return pl.pallas_call(
        _cast_kernel,
        out_shape=jax.ShapeDtypeStruct((m, d), BF16),
        grid=(m // bt,),
        in_specs=[pl.BlockSpec((bt, d), lambda i: (i, 0))],
        out_specs=pl.BlockSpec((bt, d), lambda i: (i, 0)),
        compiler_params=_params(("parallel",), 2 * bt * d * 6),
        name="cast_bf16",
    )(x)


def _mm_kernel(*refs, k_splits):
    n_a = len(k_splits)
    a_refs, w_ref, o_ref, wb_ref = refs[:n_a], refs[n_a], refs[n_a + 1], refs[n_a + 2]

    @pl.when(pl.program_id(1) == 0)
    def _cast_weights():
        wb_ref[...] = w_ref[...].astype(BF16)

    acc = None
    k0 = 0
    for a_ref, kk in zip(a_refs, k_splits):
        part = jnp.dot(a_ref[...], wb_ref[k0:k0 + kk, :], preferred_element_type=F32)
        acc = part if acc is None else acc + part
        k0 += kk
    o_ref[...] = acc.astype(o_ref.dtype)


def matmul(a_list, w, layer, ncols, *, bm=1024, bn=512, out_dtype=F32, name="proj"):
    m = a_list[0].shape[0]
    k_splits = tuple(a.shape[1] for a in a_list)
    k = sum(k_splits)
    assert w.shape[1] == k and ncols % bn == 0
    bm = min(bm, m)
    assert m % bm == 0
    in_specs = [pl.BlockSpec((bm, kk), lambda j, i: (i, 0)) for kk in k_splits]
    in_specs.append(pl.BlockSpec((None, k, bn), lambda j, i: (layer, 0, j)))
    vmem = 2 * bm * k * 2 + 2 * k * bn * 4 + k * bn * 2 + 2 * bm * bn * 4
    return pl.pallas_call(
        functools.partial(_mm_kernel, k_splits=k_splits),
        out_shape=jax.ShapeDtypeStruct((m, ncols), out_dtype),
        grid=(ncols // bn, m // bm),
        in_specs=in_specs,
        out_specs=pl.BlockSpec((bm, bn), lambda j, i: (i, j)),
        scratch_shapes=[pltpu.VMEM((k, bn), BF16)],
        compiler_params=_params(("arbitrary", "arbitrary"), vmem),
        name=name,
    )(*a_list, w)


def _mm_nt_kernel(*refs, row_shift):
    if row_shift:
        a_ref, w_ref, w_tail_ref, o_ref, wb_ref = refs
    else:
        a_ref, w_ref, o_ref, wb_ref = refs

    @pl.when(pl.program_id(1) == 0)
    def _cast_weights():
        bn, k = wb_ref.shape
        if not row_shift:
            wb_ref[...] = w_ref[...].astype(BF16)
        else:
            for c in range(k // MM_CAST_COLS):
                cols = slice(c * MM_CAST_COLS, (c + 1) * MM_CAST_COLS)
                both = jnp.concatenate([w_ref[:, cols], w_tail_ref[:, cols]], axis=0)
                wb_ref[:, cols] = both[row_shift:row_shift + bn].astype(BF16)

    o_ref[...] = lax.dot_general(a_ref[...], wb_ref[...], (((1,), (1,)), ((), ())),
                                 preferred_element_type=F32).astype(o_ref.dtype)


def matmul_nt(a, w_t, layer, ncols, *, row0=0, bm=1024, bn=512, out_dtype=F32, name="proj_t"):
    m, k = a.shape
    assert w_t.shape[2] == k and ncols % bn == 0 and k % MM_CAST_COLS == 0
    bm = min(bm, m)
    assert m % bm == 0
    row_block0, row_shift = divmod(row0, bn)
    assert row_shift < V7X_LANES and row_shift % V7X_SUBLANES == 0
    in_specs = [pl.BlockSpec((bm, k), lambda j, i: (i, 0)),
                pl.BlockSpec((None, bn, k), lambda j, i: (layer, j + row_block0, 0))]
    operands = [a, w_t]
    if row_shift:
        per = bn // V7X_LANES
        in_specs.append(pl.BlockSpec((None, V7X_LANES, k), lambda j, i: (layer, (j + row_block0 + 1) * per, 0)))
        operands.append(w_t)
    vmem = 2 * bm * k * 2 + 2 * k * (bn + V7X_LANES) * 4 + k * bn * 2 + 2 * bm * bn * 4
    return pl.pallas_call(
        functools.partial(_mm_nt_kernel, row_shift=row_shift),
        out_shape=jax.ShapeDtypeStruct((m, ncols), out_dtype),
        grid=(ncols // bn, m // bm),
        in_specs=in_specs,
        out_specs=pl.BlockSpec((bm, bn), lambda j, i: (i, j)),
        scratch_shapes=[pltpu.VMEM((bn, k), BF16)],
        compiler_params=_params(("arbitrary", "arbitrary"), vmem),
        name=name,
    )(*operands)


def _pack_bf16_pair(lo, hi):
    lo_bits = lax.bitcast_convert_type(lo.astype(BF16).astype(F32), U32)
    hi_bits = lax.bitcast_convert_type(hi.astype(BF16).astype(F32), U32)
    return (lo_bits >> 16) | (hi_bits & U32(HIGH_HALF))


def _unpack_bf16_pair(w):
    return lax.bitcast_convert_type(w << 16, F32), lax.bitcast_convert_type(w & U32(HIGH_HALF), F32)


def _layer_norm_halves(h_lo, h_hi, g, b):
    half = h_lo.shape[1]
    n = 2 * half
    mu = (jnp.sum(h_lo, -1, keepdims=True) + jnp.sum(h_hi, -1, keepdims=True)) / n
    d_lo, d_hi = h_lo - mu, h_hi - mu
    var = (jnp.sum(d_lo * d_lo, -1, keepdims=True) + jnp.sum(d_hi * d_hi, -1, keepdims=True)) / n
    inv = lax.rsqrt(var + LN_EPS)
    return d_lo * inv * g[:, :half] + b[:, :half], d_hi * inv * g[:, half:] + b[:, half:]


def _ln_kernel(x_ref, y_ref, g_ref, b_ref, o_ref, op_ref, *, alpha):
    half = op_ref.shape[1]
    o_lo, o_hi = _layer_norm_halves(alpha * x_ref[:, :half] + y_ref[:, :half],
                                    alpha * x_ref[:, half:] + y_ref[:, half:], g_ref[...], b_ref[...])
    o_ref[:, :half] = o_lo
    o_ref[:, half:] = o_hi
    op_ref[...] = _pack_bf16_pair(o_lo, o_hi)


def residual_layer_norm(x, y, g, b, layer, alpha, bt=256):
    s, d = x.shape
    bt = min(bt, s)
    row = pl.BlockSpec((bt, d), lambda i: (i, 0))
    par = pl.BlockSpec((None, 1, d), lambda i: (layer, 0, 0))
    return pl.pallas_call(
        functools.partial(_ln_kernel, alpha=alpha),
        out_shape=(jax.ShapeDtypeStruct((s, d), F32), jax.ShapeDtypeStruct((s, d // 2), U32)),
        grid=(s // bt,),
        in_specs=[row, row, par, par],
        out_specs=(row, pl.BlockSpec((bt, d // 2), lambda i: (i, 0))),
        compiler_params=_params(("parallel",), 2 * bt * d * (4 + 4 + 4 + 2)),
        name="residual_ln",
    )(x, y, g, b)


def _dn_prep_kernel(x_ref, halo_ref, w_ref, o_ref, xs_ref, *, bt, n_q_blocks, n_qk_blocks, q_scale):
    i = pl.program_id(0)
    j = pl.program_id(1)
    halo = halo_ref[...]
    xs_ref[0:V7X_SUBLANES, :] = jnp.where(i == 0, jnp.zeros_like(halo), halo)
    xs_ref[V7X_SUBLANES:, :] = x_ref[...]
    w = w_ref[...]
    y = None
    for tap in range(CONV_K):
        start = V7X_SUBLANES - (CONV_K - 1) + tap
        term = xs_ref[start:start + bt, :] * w[tap:tap + 1, :]
        y = term if y is None else y + term
    y = y * jax.nn.sigmoid(y)
    bc = y.shape[1]
    scale = jnp.where(j < n_q_blocks, q_scale, 1.0).astype(F32)
    for c in range(bc // DN_HEAD_DIM):
        sl = slice(c * DN_HEAD_DIM, (c + 1) * DN_HEAD_DIM)
        yh = y[:, sl]
        nrm = yh * (lax.rsqrt(jnp.sum(yh * yh, -1, keepdims=True) + L2_EPS) * scale)
        o_ref[:, sl] = jnp.where(j < n_qk_blocks, nrm, yh).astype(BF16)


def dn_prep(p1, conv_w, layer, dn_width, bt=512, bc=512):
    s = p1.shape[0]
    ncols = 3 * dn_width
    bt = min(bt, s)
    assert dn_width % bc == 0 and s % bt == 0
    hb = bt // V7X_SUBLANES
    return pl.pallas_call(
        functools.partial(_dn_prep_kernel, bt=bt, n_q_blocks=dn_width // bc, n_qk_blocks=2 * dn_width // bc,
                          q_scale=DN_HEAD_DIM ** -0.5),
        out_shape=jax.ShapeDtypeStruct((s, ncols), BF16),
        grid=(s // bt, ncols // bc),
        in_specs=[
            pl.BlockSpec((bt, bc), lambda i, j: (i, j)),
            pl.BlockSpec((V7X_SUBLANES, bc), lambda i, j: (jnp.maximum(i * hb - 1, 0), j)),
            pl.BlockSpec((None, CONV_K, bc), lambda i, j: (layer, 0, j)),
        ],
        out_specs=pl.BlockSpec((bt, bc), lambda i, j: (i, j)),
        scratch_shapes=[pltpu.VMEM((bt + V7X_SUBLANES, bc), F32)],
        compiler_params=_params(("parallel", "parallel"), 4 * bt * bc * 4),
        name="dn_prep",
    )(p1, p1, conv_w)


def _softplus(x):
    return jnp.maximum(x, 0.0) + jnp.log1p(jnp.exp(-jnp.abs(x)))


def _dot32(a, b):
    return jnp.dot(a, b, preferred_element_type=F32)


def _unit_lower_inverses(a_list, row, col):
    c = a_list[0].shape[0]
    hs = range(len(a_list))
    eye = (row == col).astype(F32)
    dist = row ^ col
    n = [jnp.where(dist < DN_INV_BASE, -a, 0.0) for a in a_list]
    t = [eye + n[h] for h in hs]
    p = [n[h].astype(BF16) for h in hs]
    for _ in range(int(math.log2(DN_INV_BASE)) - 1):
        p = [_dot32(p[h], p[h]).astype(BF16) for h in hs]
        t = [t[h] + _dot32(t[h].astype(BF16), p[h]) for h in hs]
    b = DN_INV_BASE
    while b < c:
        pair = (dist >= b) & (dist < 2 * b)
        mb = [jnp.where(pair, a, 0.0).astype(BF16) for a in a_list]
        tb = [t[h].astype(BF16) for h in hs]
        tm = [_dot32(tb[h], mb[h]).astype(BF16) for h in hs]
        t = [t[h] - _dot32(tm[h], tb[h]) for h in hs]
        b *= 2
    return t


def _dn_kernel(q_ref, k_ref, v_ref, z_ref, ac_ref, bc_ref, ar_ref, br_ref, alc_ref, dtc_ref, alr_ref, dtr_ref,
               nw_ref, o_ref, state_ref, *, hb, chunk, n_chunks):
    c_ = chunk
    d_ = DN_HEAD_DIM

    @pl.when(pl.program_id(1) == 0)
    def _reset():
        state_ref[...] = jnp.zeros_like(state_ref)

    row = lax.broadcasted_iota(I32, (c_, c_), 0)
    col = lax.broadcasted_iota(I32, (c_, c_), 1)
    incl = row >= col
    strict = row > col
    ltri = incl.astype(F32)
    neg_decay_c = -jnp.exp(alc_ref[...])
    neg_decay_r = -jnp.exp(alr_ref[...])
    nw = nw_ref[...]
    contract_last = (((1,), (1,)), ((), ()))
    contract_first = (((0,), (0,)), ((), ()))

    def chunk_body(ci, carry):
        r0 = pl.multiple_of(ci * c_, c_)
        rows = pl.ds(r0, c_)
        g_c = neg_decay_c * _softplus(ac_ref[rows, :] + dtc_ref[...])
        beta_c = jax.nn.sigmoid(bc_ref[rows, :])
        cum_c = jnp.dot(ltri, g_c, precision=HIGHEST, preferred_element_type=F32)
        g_r = neg_decay_r * _softplus(ar_ref[ci] + dtr_ref[...])
        cum_r = lax.dot_general(g_r, ltri, contract_last, precision=HIGHEST, preferred_element_type=F32)
        hs = range(hb)
        cols = [slice(hh * d_, (hh + 1) * d_) for hh in hs]
        gc = [cum_c[:, hh:hh + 1] for hh in hs]
        gr = [cum_r[hh:hh + 1, :] for hh in hs]
        bcol = [beta_c[:, hh:hh + 1] for hh in hs]
        qh = [q_ref[rows, cols[hh]] for hh in hs]
        kh = [k_ref[rows, cols[hh]] for hh in hs]
        vh = [v_ref[rows, cols[hh]] for hh in hs]
        kk = [lax.dot_general(kh[hh], kh[hh], contract_last, preferred_element_type=F32) for hh in hs]
        qk = [lax.dot_general(qh[hh], kh[hh], contract_last, preferred_element_type=F32) for hh in hs]
        decay = [jnp.exp(jnp.where(incl, gc[hh] - gr[hh], -jnp.inf)) for hh in hs]
        a = [jnp.where(strict, bcol[hh] * kk[hh] * decay[hh], 0.0) for hh in hs]
        t = _unit_lower_inverses(a, row, col)
        e_g = [jnp.exp(gc[hh]) for hh in hs]
        kf = [kh[hh].astype(F32) for hh in hs]
        rhs = [jnp.concatenate([vh[hh].astype(F32) * bcol[hh], kf[hh] * (bcol[hh] * e_g[hh])], axis=1).astype(BF16)
               for hh in hs]
        uw = [_dot32(t[hh].astype(BF16), rhs[hh]) for hh in hs]
        state = [state_ref[hh] for hh in hs]
        wq = [jnp.concatenate([uw[hh][:, d_:].astype(BF16), (qh[hh].astype(F32) * e_g[hh]).astype(BF16)], axis=0)
              for hh in hs]
        ws = [_dot32(wq[hh], state[hh].astype(BF16)) for hh in hs]
        v_new = [(uw[hh][:, :d_] - ws[hh][:c_]).astype(BF16) for hh in hs]
        o = [ws[hh][c_:] + _dot32((qk[hh] * decay[hh]).astype(BF16), v_new[hh]) for hh in hs]
        g_last = [gc[hh][c_ - 1:c_, :] for hh in hs]
        k_dec = [(kf[hh] * jnp.exp(g_last[hh] - gc[hh])).astype(BF16) for hh in hs]
        for hh in hs:
            state_ref[hh] = state[hh] * jnp.exp(g_last[hh]) + lax.dot_general(
                k_dec[hh], v_new[hh], contract_first, preferred_element_type=F32)
        for hh in hs:
            zf = z_ref[rows, cols[hh]]
            on = o[hh] * lax.rsqrt(jnp.mean(o[hh] * o[hh], -1, keepdims=True) + RMS_EPS) * nw
            o_ref[rows, cols[hh]] = (on * (zf * jax.nn.sigmoid(zf))).astype(BF16)
        return carry

    lax.fori_loop(0, n_chunks, chunk_body, 0)


def deltanet(qkv, p1, gates_col, gates_row, a_log, dt_bias, norm_w, layer, dn_width, bs=1024):
    s = qkv.shape[0]
    hb = DN_HEADS_PER_STEP
    c_ = DN_CHUNK
    heads = dn_width // DN_HEAD_DIM
    assert heads % hb == 0
    groups = heads // hb
    bs = min(bs, s)
    assert s % bs == 0 and bs % c_ == 0
    wblk = hb * DN_HEAD_DIM
    ncb = dn_width // wblk
    n_chunks = bs // c_
    a_log_c = a_log.reshape(-1, groups, 1, hb)
    a_log_r = a_log.reshape(-1, groups, hb, 1)
    dt_c = dt_bias.reshape(-1, groups, 1, hb)
    dt_r = dt_bias.reshape(-1, groups, hb, 1)
    nw = norm_w.reshape(-1, 1, DN_HEAD_DIM)

    def seg(which):
        return pl.BlockSpec((bs, wblk), lambda g, i: (i, which * ncb + g))

    col_spec = lambda w: pl.BlockSpec((None, None, bs, hb), lambda g, i: (w, g, i, 0))
    row_spec = lambda w: pl.BlockSpec((None, None, n_chunks, hb, c_), lambda g, i: (w, g, i, 0, 0))
    pc_spec = pl.BlockSpec((None, None, 1, hb), lambda g, i: (layer, g, 0, 0))
    pr_spec = pl.BlockSpec((None, None, hb, 1), lambda g, i: (layer, g, 0, 0))
    vmem = 2 * bs * wblk * (2 * 3 + 4 + 2) + hb * DN_HEAD_DIM * DN_HEAD_DIM * 4 + 16 * bs * V7X_LANES * 4
    return pl.pallas_call(
        functools.partial(_dn_kernel, hb=hb, chunk=c_, n_chunks=n_chunks),
        out_shape=jax.ShapeDtypeStruct((s, dn_width), BF16),
        grid=(groups, s // bs),
        in_specs=[seg(0), seg(1), seg(2), seg(3),
                  col_spec(0), col_spec(1), row_spec(0), row_spec(1),
                  pc_spec, pc_spec, pr_spec, pr_spec,
                  pl.BlockSpec((None, 1, DN_HEAD_DIM), lambda g, i: (layer, 0, 0))],
        out_specs=pl.BlockSpec((bs, wblk), lambda g, i: (i, g)),
        scratch_shapes=[pltpu.VMEM((hb, DN_HEAD_DIM, DN_HEAD_DIM), F32)],
        compiler_params=_params(("parallel", "arbitrary"), vmem),
        name="deltanet",
    )(qkv, qkv, qkv, p1, gates_col, gates_col, gates_row, gates_row, a_log_c, dt_c, a_log_r, dt_r, nw)


def _da_prep_kernel(q_ref, k_ref, v_ref, cos_ref, sin_ref, qo_ref, ko_ref, vo_ref, *, q_scale):
    c = cos_ref[...]
    s = sin_ref[...]
    half = DA_HEAD_DIM // 2
    for j in range(q_ref.shape[1] // DA_HEAD_DIM):
        sl = slice(j * DA_HEAD_DIM, (j + 1) * DA_HEAD_DIM)
        xq = q_ref[:, sl]
        xk = k_ref[:, sl]
        qo_ref[:, sl] = ((xq * c + pltpu.roll(xq, half, 1) * s) * q_scale).astype(BF16)
        ko_ref[:, sl] = (xk * c + pltpu.roll(xk, half, 1) * s).astype(BF16)
    vo_ref[...] = v_ref[...].astype(BF16)


def da_prep(p2, cos, sin_signed, da_width, bt=256):
    s = p2.shape[0]
    bt = min(bt, s)
    seg = lambda w: pl.BlockSpec((bt, da_width), lambda i: (i, w))
    tab = pl.BlockSpec((bt, DA_HEAD_DIM), lambda i: (i, 0))
    out = jax.ShapeDtypeStruct((s, da_width), BF16)
    return pl.pallas_call(
        functools.partial(_da_prep_kernel, q_scale=DA_HEAD_DIM ** -0.5 * LOG2E),
        out_shape=(out, out, out),
        grid=(s // bt,),
        in_specs=[seg(0), seg(1), seg(2), tab, tab],
        out_specs=(seg(0), seg(0), seg(0)),
        compiler_params=_params(("parallel",), 2 * 3 * bt * da_width * 6),
        name="da_prep",
    )(p2, p2, p2, cos, sin_signed)


def _da_kernel(lam_init_ref, q_ref, k_ref, v_ref, lq1_ref, lk1_ref, lq2_ref, lk2_ref, nw_ref, o_ref,
               acc_ref, m_ref, l_ref, s_a, s_b, *, blk):
    iq = pl.program_id(1)
    d_ = DA_HEAD_DIM
    rc = DA_ROW_CHUNK
    contract_last = (((1,), (1,)), ((), ()))
    m_ref[...] = jnp.full_like(m_ref, -jnp.inf)
    l_ref[...] = jnp.zeros_like(l_ref)
    acc_ref[...] = jnp.zeros_like(acc_ref)
    chains = [(c, sub) for c in range(blk // rc) for sub in range(2)]
    rsl = lambda c: slice(c * rc, (c + 1) * rc)
    lanes = lambda x, width: jnp.concatenate([x] * (width // V7X_LANES), axis=1)

    def scores(j, s_out):
        kb = k_ref[pl.ds(pl.multiple_of(j * blk, blk), blk), :]
        for n, (c, sub) in enumerate(chains):
            s_out[n] = lax.dot_general(q_ref[rsl(c), sub * d_:(sub + 1) * d_], kb[:, sub * d_:(sub + 1) * d_],
                                       contract_last, preferred_element_type=F32)

    def kv_step(j, s_cur, s_next, masked):
        if s_next is not None:
            scores(j + 1, s_next)
        vb = v_ref[pl.ds(pl.multiple_of(j * blk, blk), blk), :]
        s = [s_cur[n] for n in range(len(chains))]
        if masked:
            row = lax.broadcasted_iota(I32, (rc, blk), 0)
            col = lax.broadcasted_iota(I32, (rc, blk), 1)
            s = [jnp.where(col <= row + c * rc, s[n], -jnp.inf) for n, (c, sub) in enumerate(chains)]
        m_old = [m_ref[sub, rsl(c), :] for c, sub in chains]
        m_new = [jnp.maximum(m_old[n], jnp.max(s[n], -1, keepdims=True)) for n in range(len(chains))]
        alpha = [jnp.exp2(m_old[n] - m_new[n]) for n in range(len(chains))]
        p = [jnp.exp2(s[n] - lanes(m_new[n], blk)) for n in range(len(chains))]
        for n, (c, sub) in enumerate(chains):
            m_ref[sub, rsl(c), :] = m_new[n]
            l_ref[sub, rsl(c), :] = alpha[n] * l_ref[sub, rsl(c), :] + jnp.sum(p[n], -1, keepdims=True)
        pv = [jnp.dot(p[n].astype(BF16), vb, preferred_element_type=F32) for n in range(len(chains))]
        for n, (c, sub) in enumerate(chains):
            acc_ref[sub, rsl(c), :] = lanes(alpha[n], DA_V_DIM) * acc_ref[sub, rsl(c), :] + pv[n]

    odd = lax.rem(iq, 2)

    @pl.when(odd == 0)
    def _even_start():
        scores(0, s_a)

    @pl.when(odd == 1)
    def _odd_start():
        scores(0, s_b)
        kv_step(0, s_b, s_a, False)

    def pair(jj, carry):
        j = odd + 2 * jj
        kv_step(j, s_a, s_b, False)
        kv_step(j + 1, s_b, s_a, False)
        return carry

    lax.fori_loop(0, (iq - odd) // 2, pair, 0)
    kv_step(iq, s_a, None, True)

    lam_init = lam_init_ref[0]
    lam = (jnp.exp(jnp.sum(lq1_ref[...] * lk1_ref[...], keepdims=True))
           - jnp.exp(jnp.sum(lq2_ref[...] * lk2_ref[...], keepdims=True)) + lam_init)
    o = acc_ref[0] / lanes(l_ref[0], DA_V_DIM) - lam * (acc_ref[1] / lanes(l_ref[1], DA_V_DIM))
    on = o * lax.rsqrt(jnp.mean(o * o, -1, keepdims=True) + RMS_EPS) * nw_ref[...]
    o_ref[...] = (on * (1.0 - lam_init)).astype(BF16)


def diff_attention(qr, kr, vb, lq1, lk1, lq2, lk2, norm_w, lam_init, layer, blk=512):
    s, width = qr.shape
    heads = width // DA_V_DIM
    blk = min(blk, s)
    assert s % blk == 0
    vec = pl.BlockSpec((None, 1, DA_HEAD_DIM), lambda h, i, *_: (layer, 0, 0))
    full = pl.BlockSpec((s, DA_V_DIM), lambda h, i, *_: (0, h))
    tile = pl.BlockSpec((blk, DA_V_DIM), lambda h, i, *_: (i, h))
    grid_spec = pltpu.PrefetchScalarGridSpec(
        num_scalar_prefetch=1,
        grid=(heads, s // blk),
        in_specs=[tile, full, full, vec, vec, vec, vec,
                  pl.BlockSpec((None, 1, DA_V_DIM), lambda h, i, *_: (layer, 0, 0))],
        out_specs=tile,
        scratch_shapes=[pltpu.VMEM((2, blk, DA_V_DIM), F32), pltpu.VMEM((2, blk, V7X_LANES), F32),
                        pltpu.VMEM((2, blk, V7X_LANES), F32),
                        pltpu.VMEM((2 * blk // DA_ROW_CHUNK, DA_ROW_CHUNK, blk), F32),
                        pltpu.VMEM((2 * blk // DA_ROW_CHUNK, DA_ROW_CHUNK, blk), F32)],
    )
    vmem = 4 * s * DA_V_DIM * 2 + 4 * blk * DA_V_DIM * 2 + 2 * blk * DA_V_DIM * 4 + 8 * blk * blk * 4
    return pl.pallas_call(
        functools.partial(_da_kernel, blk=blk),
        out_shape=jax.ShapeDtypeStruct((s, width), BF16),
        grid_spec=grid_spec,
        compiler_params=_params(("parallel", "arbitrary"), vmem),
        name="diff_attention",
    )(lam_init, qr, kr, vb, lq1, lk1, lq2, lk2, norm_w)


def _mem_attn_kernel(q_ref, k_ref, v_ref, o_ref, *, scale):
    q = (q_ref[...] * scale).astype(BF16)
    k = k_ref[...].astype(BF16)
    sc = lax.dot_general(q, k, (((1,), (1,)), ((), ())), preferred_element_type=F32)
    sc = sc - jnp.max(sc, -1, keepdims=True)
    p = jnp.exp(sc)
    denom = jnp.sum(p, -1, keepdims=True)
    o = jnp.dot(p.astype(BF16), v_ref[...].astype(BF16), preferred_element_type=F32) / denom
    o_ref[...] = o.astype(BF16)


def memory_attention(p2, mem_kv, q_col0, mem_width, bt=512):
    s = p2.shape[0]
    n_mem = mem_kv.shape[0]
    hd = mem_width // MEM_HEADS
    bt = min(bt, s)
    assert q_col0 % hd == 0
    qb0 = q_col0 // hd
    return pl.pallas_call(
        functools.partial(_mem_attn_kernel, scale=hd ** -0.5),
        out_shape=jax.ShapeDtypeStruct((s, mem_width), BF16),
        grid=(s // bt, MEM_HEADS),
        in_specs=[pl.BlockSpec((bt, hd), lambda i, h: (i, qb0 + h)),
                  pl.BlockSpec((n_mem, hd), lambda i, h: (0, h)),
                  pl.BlockSpec((n_mem, hd), lambda i, h: (0, MEM_HEADS + h))],
        out_specs=pl.BlockSpec((bt, hd), lambda i, h: (i, h)),
        compiler_params=_params(("parallel", "parallel"), 4 * bt * hd * 4 + 4 * n_mem * hd * 4 + 4 * bt * n_mem * 4),
        name="memory_attention",
    )(p2, mem_kv, mem_kv)


def _router_kernel(x_ref, w_ref, b_ref, idx_ref, gate_ref, rank_ref, cnt_ref, carry_ref):
    i = pl.program_id(0)

    @pl.when(i == 0)
    def _reset():
        carry_ref[...] = jnp.zeros_like(carry_ref)

    logits = jnp.dot(x_ref[...], w_ref[...], precision=HIGHEST, preferred_element_type=F32) + b_ref[...]
    bt, n_e = logits.shape
    lane = lax.broadcasted_iota(I32, (bt, n_e), 1)
    work = logits
    vals, sels = [], []
    for kk in range(TOP_K):
        mx = jnp.max(work, -1, keepdims=True)
        idx = jnp.min(jnp.where(work == mx, lane, n_e), -1, keepdims=True)
        sel = lane == idx
        vals.append(mx)
        sels.append(sel)
        idx_ref[:, kk:kk + 1] = idx
        work = jnp.where(sel, -jnp.inf, work)
    exps = [jnp.exp(v - vals[0]) for v in vals]
    denom = exps[0]
    for e in exps[1:]:
        denom = denom + e
    for kk in range(TOP_K):
        gate_ref[:, kk:kk + 1] = exps[kk] / denom
    mask = sels[0]
    for sel in sels[1:]:
        mask = mask | sel
    maskf = mask.astype(F32)
    row = lax.broadcasted_iota(I32, (bt, bt), 0)
    col = lax.broadcasted_iota(I32, (bt, bt), 1)
    before = (row > col).astype(BF16)
    rank = carry_ref[...] + jnp.dot(before, maskf.astype(BF16), preferred_element_type=F32)
    for kk in range(TOP_K):
        rank_ref[:, kk:kk + 1] = jnp.sum(jnp.where(sels[kk], rank, 0.0), -1, keepdims=True).astype(I32)
    carry_ref[...] = carry_ref[...] + jnp.sum(maskf, 0, keepdims=True)
    cnt_ref[...] = carry_ref[...].astype(I32)


def router(x, router_w, router_b, layer, bt=256):
    s, d = x.shape
    n_e = router_w.shape[-1]
    bt = min(bt, s)
    small = lambda dt: jax.ShapeDtypeStruct((s, TOP_K), dt)
    tok = pl.BlockSpec((bt, TOP_K), lambda i: (i, 0))
    return pl.pallas_call(
        _router_kernel,
        out_shape=(small(I32), small(F32), small(I32), jax.ShapeDtypeStruct((1, n_e), I32)),
        grid=(s // bt,),
        in_specs=[pl.BlockSpec((bt, d), lambda i: (i, 0)),
                  pl.BlockSpec((None, d, n_e), lambda i: (layer, 0, 0)),
                  pl.BlockSpec((None, 1, n_e), lambda i: (layer, 0, 0))],
        out_specs=(tok, tok, tok, pl.BlockSpec((1, n_e), lambda i: (0, 0))),
        scratch_shapes=[pltpu.VMEM((1, n_e), F32)],
        compiler_params=_params(("arbitrary",), 2 * bt * d * 4 + 2 * d * V7X_LANES * 4),
        name="router",
    )(x, router_w, router_b)


def _row_copy(src, src_row, dst, dst_row, sem):
    return pltpu.make_async_copy(src.at[pl.ds(src_row, 1)], dst.at[pl.ds(dst_row, 1)], sem)


def _moe_kernel(te_ref, nt_ref, src_ref, src_next_ref, dst_ref, x_hbm, wg_ref, wu_ref, wd_ref, bg_ref, bu_ref,
                bd_ref, y_hbm, xbuf, ybuf, wgb, wub, wdb, gsem, ssem, *, tm):
    i = pl.program_id(0)
    nt = nt_ref[0]
    slot = lax.rem(i, 2)
    other = 1 - slot

    def start_gather(idx_ref, s_):
        for r in range(tm):
            _row_copy(x_hbm, idx_ref[0, r], xbuf.at[s_], r, gsem.at[s_]).start()

    def start_scatter(s_):
        for r in range(tm):
            _row_copy(ybuf.at[s_], r, y_hbm, dst_ref[0, r], ssem.at[s_]).start()

    def wait_gather(s_):
        pltpu.make_async_copy(x_hbm.at[pl.ds(0, tm)], xbuf.at[s_], gsem.at[s_]).wait()

    def wait_scatter(s_):
        pltpu.make_async_copy(ybuf.at[s_], y_hbm.at[pl.ds(0, tm)], ssem.at[s_]).wait()

    @pl.when(i == 0)
    def _init_spare_rows():
        ybuf[other] = jnp.zeros((tm, ybuf.shape[2]), U32)
        spare = pltpu.make_async_copy(ybuf.at[other], y_hbm.at[pl.ds(y_hbm.shape[0] - tm, tm)], ssem.at[other])
        spare.start()
        spare.wait()

    @pl.when((i == 0) & (nt > 0))
    def _first_gather():
        start_gather(src_ref, slot)

    @pl.when(i < nt)
    def _tile():
        wait_gather(slot)

        @pl.when(i + 1 < nt)
        def _next_gather():
            start_gather(src_next_ref, other)

        prev = te_ref[jnp.maximum(i - 1, 0)]

        @pl.when((i == 0) | (te_ref[i] != prev))
        def _cast_weights():
            wgb[...] = wg_ref[...].astype(BF16)
            wub[...] = wu_ref[...].astype(BF16)
            wdb[...] = wd_ref[...].astype(BF16)

        half = xbuf.shape[2]
        x_lo, x_hi = _unpack_bf16_pair(xbuf[slot])
        x_lo, x_hi = x_lo.astype(BF16), x_hi.astype(BF16)
        gate = _dot32(x_lo, wgb[:half, :]) + _dot32(x_hi, wgb[half:, :]) + bg_ref[...]
        up = _dot32(x_lo, wub[:half, :]) + _dot32(x_hi, wub[half:, :]) + bu_ref[...]
        gate = jnp.minimum(gate, SWIGLU_LIMIT)
        up = jnp.clip(up, -SWIGLU_LIMIT, SWIGLU_LIMIT)
        act = ((up + 1.0) * gate * jax.nn.sigmoid(SWIGLU_ALPHA * gate)).astype(BF16)
        y_lo = _dot32(act, wdb[:, :half]) + bd_ref[:, :half]
        y_hi = _dot32(act, wdb[:, half:]) + bd_ref[:, half:]
        ybuf[slot] = _pack_bf16_pair(y_lo, y_hi)

        @pl.when(i > 0)
        def _drain_previous():
            wait_scatter(other)

        start_scatter(slot)

        @pl.when(i == nt - 1)
        def _drain_last():
            wait_scatter(slot)


def routed_experts(x, tile_expert, n_tiles, src_tok, dst_row, w_gate, b_gate, w_up, b_up, w_down, b_down, layer):
    s, half = x.shape
    d = 2 * half
    f = w_gate.shape[-1]
    tm = MOE_TILE
    nt = tile_expert.shape[0]
    idx_spec = pl.BlockSpec((None, 1, tm), lambda i, *_: (i, 0, 0), memory_space=pltpu.SMEM)
    idx_next = pl.BlockSpec((None, 1, tm), lambda i, *_: (jnp.minimum(i + 1, nt - 1), 0, 0),
                            memory_space=pltpu.SMEM)
    wspec = lambda a, b: pl.BlockSpec((None, None, a, b), lambda i, te, n: (layer, te[i], 0, 0))
    grid_spec = pltpu.PrefetchScalarGridSpec(
        num_scalar_prefetch=2,
        grid=(nt,),
        in_specs=[idx_spec, idx_next, idx_spec, pl.BlockSpec(memory_space=pl.ANY),
                  wspec(d, f), wspec(d, f), wspec(f, d), wspec(1, f), wspec(1, f), wspec(1, d)],
        out_specs=pl.BlockSpec(memory_space=pl.ANY),
        scratch_shapes=[pltpu.VMEM((2, tm, half), U32), pltpu.VMEM((2, tm, half), U32),
                        pltpu.VMEM((d, f), BF16), pltpu.VMEM((d, f), BF16), pltpu.VMEM((f, d), BF16),
                        pltpu.SemaphoreType.DMA((2,)), pltpu.SemaphoreType.DMA((2,))],
    )
    vmem = 2 * 3 * d * f * 4 + 3 * d * f * 2 + 4 * tm * half * 4 + 4 * tm * d * 4
    return pl.pallas_call(
        functools.partial(_moe_kernel, tm=tm),
        out_shape=jax.ShapeDtypeStruct((TOP_K * s + tm, half), U32),
        grid_spec=grid_spec,
        compiler_params=_params(("arbitrary",), vmem),
        name="routed_experts",
    )(tile_expert, n_tiles, src_tok, src_tok, dst_row, x, w_gate, w_up, w_down, b_gate, b_up, b_down)


def _combine_ln_kernel(x_ref, y0_ref, y1_ref, y2_ref, y3_ref, gate_ref, g_ref, b_ref, o_ref, ob_ref, *, alpha):
    half = y0_ref.shape[1]
    gates = gate_ref[...]
    ffn_lo = ffn_hi = None
    for kk, y_ref in enumerate((y0_ref, y1_ref, y2_ref, y3_ref)):
        y_lo, y_hi = _unpack_bf16_pair(y_ref[...])
        gk = gates[:, kk:kk + 1]
        ffn_lo = gk * y_lo if ffn_lo is None else ffn_lo + gk * y_lo
        ffn_hi = gk * y_hi if ffn_hi is None else ffn_hi + gk * y_hi
    o_lo, o_hi = _layer_norm_halves(alpha * x_ref[:, :half] + ffn_lo, alpha * x_ref[:, half:] + ffn_hi,
                                    g_ref[...], b_ref[...])
    o_ref[:, :half] = o_lo
    o_ref[:, half:] = o_hi
    ob_ref[:, :half] = o_lo.astype(BF16)
    ob_ref[:, half:] = o_hi.astype(BF16)


def combine_layer_norm(x, y, gates, g, b, layer, alpha, bt=256):
    s, d = x.shape
    half = d // 2
    bt = min(bt, s)
    nb = s // bt
    row = pl.BlockSpec((bt, d), lambda i: (i, 0))
    yk = lambda kk: pl.BlockSpec((bt, half), lambda i: (kk * nb + i, 0))
    par = pl.BlockSpec((None, 1, d), lambda i: (layer, 0, 0))
    return pl.pallas_call(
        functools.partial(_combine_ln_kernel, alpha=alpha),
        out_shape=(jax.ShapeDtypeStruct((s, d), F32), jax.ShapeDtypeStruct((s, d), BF16)),
        grid=(nb,),
        in_specs=[row, yk(0), yk(1), yk(2), yk(3), pl.BlockSpec((bt, TOP_K), lambda i: (i, 0)), par, par],
        out_specs=(row, row),
        compiler_params=_params(("parallel",), 2 * bt * d * (4 + 4 * 2 + 4 + 2)),
        name="combine_ln",
    )(x, y, y, y, y, gates, g, b)


def _routing_plan(idx, rank, counts, s, n_e):
    tm = MOE_TILE
    nt = s * TOP_K // tm + n_e
    counts = counts.reshape(n_e)
    tiles_e = (counts + tm - 1) // tm
    tile_end = jnp.cumsum(tiles_e)
    tile_start = tile_end - tiles_e
    n_tiles = tile_end[-1]
    tile_ids = jnp.arange(nt, dtype=I32)
    te = jnp.sum((tile_end[None, :] <= jnp.minimum(tile_ids, n_tiles - 1)[:, None]).astype(I32), axis=1)
    te = jnp.clip(te, 0, n_e - 1).astype(I32)
    pos = (tile_start * tm)[idx] + rank
    tok = jnp.arange(s, dtype=I32)[:, None]
    val = jnp.arange(TOP_K, dtype=I32)[None, :] * s + tok
    spare = jnp.tile(TOP_K * s + jnp.arange(tm, dtype=I32), nt)
    slots = spare.at[pos.reshape(-1)].set(val.reshape(-1), unique_indices=True)
    dst_row = slots.reshape(nt, 1, tm)
    src_tok = (slots % s).reshape(nt, 1, tm)
    return te, n_tiles.reshape(1).astype(I32), src_tok, dst_row


def kernel(x, mem, positions, w_in, conv_w, a_log, dt_bias, dn_norm_w, lambda_q1, lambda_k1, lambda_q2,
           lambda_k2, da_norm_w, w_mem_kv, w_o, ln1_g, ln1_b, router_w, router_b, w_gate, b_gate, w_up, b_up,
           w_down, b_down, ln2_g, ln2_b):
    bsz, s, d = x.shape
    assert bsz == 1
    depth = w_in.shape[0]
    n_e = router_w.shape[-1]
    f = w_gate.shape[-1]
    dn_w = 3 * d // 8
    da_w = 3 * d // 8
    mem_w = d - dn_w - da_w
    dn_heads = dn_w // DN_HEAD_DIM
    alpha = (2 * depth) ** 0.25
    hb = DN_HEADS_PER_STEP
    groups = dn_heads // hb
    c_ = DN_CHUNK

    inv = 1.0 / (ROPE_THETA ** (jnp.arange(0, DA_HEAD_DIM, 2, dtype=F32) / DA_HEAD_DIM))
    ang = positions[0].astype(F32)[:, None] * inv
    ang = jnp.concatenate([ang, ang], -1)
    sign = jnp.concatenate([-jnp.ones((DA_HEAD_DIM // 2,), F32), jnp.ones((DA_HEAD_DIM // 2,), F32)])
    cos, sin_signed = jnp.cos(ang), jnp.sin(ang) * sign
    gate0 = 4 * dn_w
    post0 = gate0 + 2 * dn_heads
    w_in_t = jnp.swapaxes(w_in, 1, 2)
    lam_inits = [0.8 - 0.6 * math.exp(-0.3 * layer) for layer in range(depth)]
    vec3 = lambda p: p.reshape(depth, 1, -1)
    ln1_g3, ln1_b3, ln2_g3, ln2_b3 = vec3(ln1_g), vec3(ln1_b), vec3(ln2_g), vec3(ln2_b)
    lq1, lk1, lq2, lk2 = vec3(lambda_q1), vec3(lambda_k1), vec3(lambda_q2), vec3(lambda_k2)
    da_nw = vec3(da_norm_w)
    router_b3 = vec3(router_b)
    b_gate4 = b_gate.reshape(depth, n_e, 1, f)
    b_up4 = b_up.reshape(depth, n_e, 1, f)
    b_down4 = b_down.reshape(depth, n_e, 1, d)

    xf = x[0]
    xb = cast_bf16(xf)
    memb = cast_bf16(mem[0])
    for layer in range(depth):
        p1 = matmul_nt(xb, w_in_t, layer, 4 * dn_w, name="proj_dn")
        p2 = matmul_nt(xb, w_in_t, layer, 3 * da_w + mem_w, row0=post0, name="proj_attn")
        gts = matmul_nt(xb, w_in_t, layer, V7X_LANES, row0=gate0, bn=V7X_LANES, name="proj_gates")
        mem_kv = matmul([memb], w_mem_kv, layer, 2 * mem_w, name="proj_mem")

        ab = gts[:, :2 * dn_heads].reshape(s, 2, groups, hb)
        gates_col = ab.transpose(1, 2, 0, 3)
        gates_row = ab.reshape(s // c_, c_, 2, groups, hb).transpose(2, 3, 0, 4, 1)
        qkv = dn_prep(p1, conv_w, layer, dn_w)
        y_dn = deltanet(qkv, p1, gates_col, gates_row, a_log, dt_bias, dn_norm_w, layer, dn_w)

        qr, kr, vb = da_prep(p2, cos, sin_signed, da_w)
        lam_init = jnp.full((1,), lam_inits[layer], F32)
        y_da = diff_attention(qr, kr, vb, lq1, lk1, lq2, lk2, da_nw, lam_init, layer)
        y_mem = memory_attention(p2, mem_kv, 3 * da_w, mem_w)

        mixed = matmul([y_dn, y_da, y_mem], w_o, layer, d, name="proj_out")
        x1, x1_packed = residual_layer_norm(xf, mixed, ln1_g3, ln1_b3, layer, alpha)

        idx, gates, rank, counts = router(x1, router_w, router_b3, layer)
        te, n_tiles, src_tok, dst_row = _routing_plan(idx, rank, counts, s, n_e)
        y = routed_experts(x1_packed, te, n_tiles, src_tok, dst_row, w_gate, b_gate4, w_up, b_up4, w_down, b_down4,
                           layer)
        xf, xb = combine_layer_norm(x1, y, gates, ln2_g3, ln2_b3, layer, alpha)
    return xf[None]
```

```python
import functools
import math

import jax
import jax.numpy as jnp
from jax import lax
from jax.experimental import pallas as pl
from jax.experimental.pallas import tpu as pltpu

F32 = jnp.float32
BF16 = jnp.bfloat16
I32 = jnp.int32
U32 = jnp.uint32
HIGHEST = lax.Precision.HIGHEST
HIGH_HALF = 0xFFFF0000

V7X_LANES = 128
V7X_SUBLANES = 8
V7X_VMEM_BYTES = 64 * 1024 * 1024
VMEM_COMPILER_MARGIN = 6 * 1024 * 1024

DN_HEAD_DIM = 128
CONV_K = 4
DA_HEAD_DIM = 128
DA_V_DIM = 2 * DA_HEAD_DIM
MEM_HEADS = 4
TOP_K = 4
ROPE_THETA = 10000.0
SWIGLU_LIMIT = 7.0
SWIGLU_ALPHA = 1.702
LN_EPS = 1e-5
RMS_EPS = 1e-6
L2_EPS = 1e-6
LOG2E = 1.4426950408889634

DN_CHUNK = 128
DN_HEADS_PER_STEP = 12
DN_INV_BASE = 16
MOE_TILE = 256
MOE_DMA_GROUPS = 3
MM_CAST_COLS = 512
DA_ROW_CHUNK = 128


def _vmem_limit(nbytes):
    return int(min(nbytes + VMEM_COMPILER_MARGIN, V7X_VMEM_BYTES - 2 * 1024 * 1024))


def _params(semantics, vmem_bytes):
    return pltpu.CompilerParams(dimension_semantics=semantics, vmem_limit_bytes=_vmem_limit(vmem_bytes))


def _cast_kernel(x_ref, o_ref):
    o_ref[...] = x_ref[...].astype(o_ref.dtype)


def cast_bf16(x, bt=256):
    m, d = x.shape
    bt = min(bt, m)
    return pl.pallas_call(
        _cast_kernel,
        out_shape=jax.ShapeDtypeStruct((m, d), BF16),
        grid=(m // bt,),
        in_specs=[pl.BlockSpec((bt, d), lambda i: (i, 0))],
        out_specs=pl.BlockSpec((bt, d), lambda i: (i, 0)),
        compiler_params=_params(("parallel",), 2 * bt * d * 6),
        name="cast_bf16",
    )(x)


def _mm_kernel(*refs, k_splits):
    n_a = len(k_splits)
    a_refs, w_ref, o_ref, wb_ref = refs[:n_a], refs[n_a], refs[n_a + 1], refs[n_a + 2]

    @pl.when(pl.program_id(1) == 0)
    def _cast_weights():
        wb_ref[...] = w_ref[...].astype(BF16)

    acc = None
    k0 = 0
    for a_ref, kk in zip(a_refs, k_splits):
        part = jnp.dot(a_ref[...], wb_ref[k0:k0 + kk, :], preferred_element_type=F32)
        acc = part if acc is None else acc + part
        k0 += kk
    o_ref[...] = acc.astype(o_ref.dtype)


def matmul(a_list, w, layer, ncols, *, bm=1024, bn=512, out_dtype=F32, name="proj"):
    m = a_list[0].shape[0]
    k_splits = tuple(a.shape[1] for a in a_list)
    k = sum(k_splits)
    assert w.shape[1] == k and ncols % bn == 0
    bm = min(bm, m)
    assert m % bm == 0
    in_specs = [pl.BlockSpec((bm, kk), lambda j, i: (i, 0)) for kk in k_splits]
    in_specs.append(pl.BlockSpec((None, k, bn), lambda j, i: (layer, 0, j)))
    vmem = 2 * bm * k * 2 + 2 * k * bn * 4 + k * bn * 2 + 2 * bm * bn * 4
    return pl.pallas_call(
        functools.partial(_mm_kernel, k_splits=k_splits),
        out_shape=jax.ShapeDtypeStruct((m, ncols), out_dtype),
        grid=(ncols // bn, m // bm),
        in_specs=in_specs,
        out_specs=pl.BlockSpec((bm, bn), lambda j, i: (i, j)),
        scratch_shapes=[pltpu.VMEM((k, bn), BF16)],
        compiler_params=_params(("arbitrary", "arbitrary"), vmem),
        name=name,
    )(*a_list, w)


def _mm_nt_kernel(*refs, row_shift):
    if row_shift:
        a_ref, w_ref, w_tail_ref, o_ref, wb_ref = refs
    else:
        a_ref, w_ref, o_ref, wb_ref = refs

    @pl.when(pl.program_id(1) == 0)
    def _cast_weights():
        bn, k = wb_ref.shape
        if not row_shift:
            wb_ref[...] = w_ref[...].astype(BF16)
        else:
            for c in range(k // MM_CAST_COLS):
                cols = slice(c * MM_CAST_COLS, (c + 1) * MM_CAST_COLS)
                both = jnp.concatenate([w_ref[:, cols], w_tail_ref[:, cols]], axis=0)
                wb_ref[:, cols] = both[row_shift:row_shift + bn].astype(BF16)

    o_ref[...] = lax.dot_general(a_ref[...], wb_ref[...], (((1,), (1,)), ((), ())),
                                 preferred_element_type=F32).astype(o_ref.dtype)


def matmul_nt(a, w_t, layer, ncols, *, row0=0, bm=1024, bn=512, out_dtype=F32, name="proj_t"):
    m, k = a.shape
    assert w_t.shape[2] == k and ncols % bn == 0 and k % MM_CAST_COLS == 0
    bm = min(bm, m)
    assert m % bm == 0
    row_block0, row_shift = divmod(row0, bn)
    assert row_shift < V7X_LANES and row_shift % V7X_SUBLANES == 0
    in_specs = [pl.BlockSpec((bm, k), lambda j, i: (i, 0)),
                pl.BlockSpec((None, bn, k), lambda j, i: (layer, j + row_block0, 0))]
    operands = [a, w_t]
    if row_shift:
        per = bn // V7X_LANES
        in_specs.append(pl.BlockSpec((None, V7X_LANES, k), lambda j, i: (layer, (j + row_block0 + 1) * per, 0)))
        operands.append(w_t)
    vmem = 2 * bm * k * 2 + 2 * k * (bn + V7X_LANES) * 4 + k * bn * 2 + 2 * bm * bn * 4
    return pl.pallas_call(
        functools.partial(_mm_nt_kernel, row_shift=row_shift),
        out_shape=jax.ShapeDtypeStruct((m, ncols), out_dtype),
        grid=(ncols // bn, m // bm),
        in_specs=in_specs,
        out_specs=pl.BlockSpec((bm, bn), lambda j, i: (i, j)),
        scratch_shapes=[pltpu.VMEM((bn, k), BF16)],
        compiler_params=_params(("arbitrary", "arbitrary"), vmem),
        name=name,
    )(*operands)


def _pack_bf16_pair(lo, hi):
    lo_bits = lax.bitcast_convert_type(lo.astype(BF16).astype(F32), U32)
    hi_bits = lax.bitcast_convert_type(hi.astype(BF16).astype(F32), U32)
    return (lo_bits >> 16) | (hi_bits & U32(HIGH_HALF))


def _unpack_bf16_pair(w):
    return lax.bitcast_convert_type(w << 16, F32), lax.bitcast_convert_type(w & U32(HIGH_HALF), F32)


def _layer_norm_halves(h_lo, h_hi, g, b):
    half = h_lo.shape[1]
    n = 2 * half
    mu = (jnp.sum(h_lo, -1, keepdims=True) + jnp.sum(h_hi, -1, keepdims=True)) / n
    d_lo, d_hi = h_lo - mu, h_hi - mu
    var = (jnp.sum(d_lo * d_lo, -1, keepdims=True) + jnp.sum(d_hi * d_hi, -1, keepdims=True)) / n
    inv = lax.rsqrt(var + LN_EPS)
    return d_lo * inv * g[:, :half] + b[:, :half], d_hi * inv * g[:, half:] + b[:, half:]


def _ln_kernel(x_ref, y_ref, g_ref, b_ref, o_ref, op_ref, *, alpha):
    half = op_ref.shape[1]
    o_lo, o_hi = _layer_norm_halves(alpha * x_ref[:, :half] + y_ref[:, :half],
                                    alpha * x_ref[:, half:] + y_ref[:, half:], g_ref[...], b_ref[...])
    o_ref[:, :half] = o_lo
    o_ref[:, half:] = o_hi
    op_ref[...] = _pack_bf16_pair(o_lo, o_hi)


def residual_layer_norm(x, y, g, b, layer, alpha, bt=256):
    s, d = x.shape
    bt = min(bt, s)
    row = pl.BlockSpec((bt, d), lambda i: (i, 0))
    par = pl.BlockSpec((None, 1, d), lambda i: (layer, 0, 0))
    return pl.pallas_call(
        functools.partial(_ln_kernel, alpha=alpha),
        out_shape=(jax.ShapeDtypeStruct((s, d), F32), jax.ShapeDtypeStruct((s, d // 2), U32)),
        grid=(s // bt,),
        in_specs=[row, row, par, par],
        out_specs=(row, pl.BlockSpec((bt, d // 2), lambda i: (i, 0))),
        compiler_params=_params(("parallel",), 2 * bt * d * (4 + 4 + 4 + 2)),
        name="residual_ln",
    )(x, y, g, b)


def _dn_prep_kernel(x_ref, halo_ref, w_ref, o_ref, xs_ref, *, bt, n_q_blocks, n_qk_blocks, q_scale):
    i = pl.program_id(0)
    j = pl.program_id(1)
    halo = halo_ref[...]
    xs_ref[0:V7X_SUBLANES, :] = jnp.where(i == 0, jnp.zeros_like(halo), halo)
    xs_ref[V7X_SUBLANES:, :] = x_ref[...]
    w = w_ref[...]
    y = None
    for tap in range(CONV_K):
        start = V7X_SUBLANES - (CONV_K - 1) + tap
        term = xs_ref[start:start + bt, :] * w[tap:tap + 1, :]
        y = term if y is None else y + term
    y = y * jax.nn.sigmoid(y)
    bc = y.shape[1]
    scale = jnp.where(j < n_q_blocks, q_scale, 1.0).astype(F32)
    for c in range(bc // DN_HEAD_DIM):
        sl = slice(c * DN_HEAD_DIM, (c + 1) * DN_HEAD_DIM)
        yh = y[:, sl]
        nrm = yh * (lax.rsqrt(jnp.sum(yh * yh, -1, keepdims=True) + L2_EPS) * scale)
        o_ref[:, sl] = jnp.where(j < n_qk_blocks, nrm, yh).astype(BF16)


def dn_prep(p1, conv_w, layer, dn_width, bt=512, bc=512):
    s = p1.shape[0]
    ncols = 3 * dn_width
    bt = min(bt, s)
    assert dn_width % bc == 0 and s % bt == 0
    hb = bt // V7X_SUBLANES
    return pl.pallas_call(
        functools.partial(_dn_prep_kernel, bt=bt, n_q_blocks=dn_width // bc, n_qk_blocks=2 * dn_width // bc,
                          q_scale=DN_HEAD_DIM ** -0.5),
        out_shape=jax.ShapeDtypeStruct((s, ncols), BF16),
        grid=(s // bt, ncols // bc),
        in_specs=[
            pl.BlockSpec((bt, bc), lambda i, j: (i, j)),
            pl.BlockSpec((V7X_SUBLANES, bc), lambda i, j: (jnp.maximum(i * hb - 1, 0), j)),
            pl.BlockSpec((None, CONV_K, bc), lambda i, j: (layer, 0, j)),
        ],
        out_specs=pl.BlockSpec((bt, bc), lambda i, j: (i, j)),
        scratch_shapes=[pltpu.VMEM((bt + V7X_SUBLANES, bc), F32)],
        compiler_params=_params(("parallel", "parallel"), 4 * bt * bc * 4),
        name="dn_prep",
    )(p1, p1, conv_w)


def _softplus(x):
    return jnp.maximum(x, 0.0) + jnp.log1p(jnp.exp(-jnp.abs(x)))


def _dot32(a, b):
    return jnp.dot(a, b, preferred_element_type=F32)


def _unit_lower_inverses(a_list, row, col):
    c = a_list[0].shape[0]
    hs = range(len(a_list))
    eye = (row == col).astype(F32)
    dist = row ^ col
    n = [jnp.where(dist < DN_INV_BASE, -a, 0.0) for a in a_list]
    t = [eye + n[h] for h in hs]
    p = [n[h].astype(BF16) for h in hs]
    for _ in range(int(math.log2(DN_INV_BASE)) - 1):
        p = [_dot32(p[h], p[h]).astype(BF16) for h in hs]
        t = [t[h] + _dot32(t[h].astype(BF16), p[h]) for h in hs]
    b = DN_INV_BASE
    while b < c:
        pair = (dist >= b) & (dist < 2 * b)
        mb = [jnp.where(pair, a, 0.0).astype(BF16) for a in a_list]
        tb = [t[h].astype(BF16) for h in hs]
        tm = [_dot32(tb[h], mb[h]).astype(BF16) for h in hs]
        t = [t[h] - _dot32(tm[h], tb[h]) for h in hs]
        b *= 2
    return t


def _dn_kernel(q_ref, k_ref, v_ref, z_ref, ac_ref, bc_ref, ar_ref, br_ref, alc_ref, dtc_ref, alr_ref, dtr_ref,
               nw_ref, o_ref, state_ref, *, hb, chunk, n_chunks):
    c_ = chunk
    d_ = DN_HEAD_DIM

    @pl.when(pl.program_id(1) == 0)
    def _reset():
        state_ref[...] = jnp.zeros_like(state_ref)

    row = lax.broadcasted_iota(I32, (c_, c_), 0)
    col = lax.broadcasted_iota(I32, (c_, c_), 1)
    incl = row >= col
    strict = row > col
    ltri = incl.astype(F32)
    neg_decay_c = -jnp.exp(alc_ref[...])
    neg_decay_r = -jnp.exp(alr_ref[...])
    nw = nw_ref[...]
    contract_last = (((1,), (1,)), ((), ()))
    contract_first = (((0,), (0,)), ((), ()))

    def chunk_body(ci, carry):
        r0 = pl.multiple_of(ci * c_, c_)
        rows = pl.ds(r0, c_)
        g_c = neg_decay_c * _softplus(ac_ref[rows, :] + dtc_ref[...])
        beta_c = jax.nn.sigmoid(bc_ref[rows, :])
        cum_c = jnp.dot(ltri, g_c, precision=HIGHEST, preferred_element_type=F32)
        g_r = neg_decay_r * _softplus(ar_ref[ci] + dtr_ref[...])
        cum_r = lax.dot_general(g_r, ltri, contract_last, precision=HIGHEST, preferred_element_type=F32)
        hs = range(hb)
        cols = [slice(hh * d_, (hh + 1) * d_) for hh in hs]
        gc = [cum_c[:, hh:hh + 1] for hh in hs]
        gr = [cum_r[hh:hh + 1, :] for hh in hs]
        bcol = [beta_c[:, hh:hh + 1] for hh in hs]
        qh = [q_ref[rows, cols[hh]] for hh in hs]
        kh = [k_ref[rows, cols[hh]] for hh in hs]
        vh = [v_ref[rows, cols[hh]] for hh in hs]
        kk = [lax.dot_general(kh[hh], kh[hh], contract_last, preferred_element_type=F32) for hh in hs]
        qk = [lax.dot_general(qh[hh], kh[hh], contract_last, preferred_element_type=F32) for hh in hs]
        decay = [jnp.exp(jnp.where(incl, gc[hh] - gr[hh], -jnp.inf)) for hh in hs]
        a = [jnp.where(strict, bcol[hh] * kk[hh] * decay[hh], 0.0) for hh in hs]
        t = _unit_lower_inverses(a, row, col)
        e_g = [jnp.exp(gc[hh]) for hh in hs]
        kf = [kh[hh].astype(F32) for hh in hs]
        rhs = [jnp.concatenate([vh[hh].astype(F32) * bcol[hh], kf[hh] * (bcol[hh] * e_g[hh])], axis=1).astype(BF16)
               for hh in hs]
        uw = [_dot32(t[hh].astype(BF16), rhs[hh]) for hh in hs]
        state = [state_ref[hh] for hh in hs]
        wq = [jnp.concatenate([uw[hh][:, d_:].astype(BF16), (qh[hh].astype(F32) * e_g[hh]).astype(BF16)], axis=0)
              for hh in hs]
        ws = [_dot32(wq[hh], state[hh].astype(BF16)) for hh in hs]
        v_new = [(uw[hh][:, :d_] - ws[hh][:c_]).astype(BF16) for hh in hs]
        o = [ws[hh][c_:] + _dot32((qk[hh] * decay[hh]).astype(BF16), v_new[hh]) for hh in hs]
        g_last = [gc[hh][c_ - 1:c_, :] for hh in hs]
        k_dec = [(kf[hh] * jnp.exp(g_last[hh] - gc[hh])).astype(BF16) for hh in hs]
        for hh in hs:
            state_ref[hh] = state[hh] * jnp.exp(g_last[hh]) + lax.dot_general(
                k_dec[hh], v_new[hh], contract_first, preferred_element_type=F32)
        for hh in hs:
            zf = z_ref[rows, cols[hh]]
            on = o[hh] * lax.rsqrt(jnp.mean(o[hh] * o[hh], -1, keepdims=True) + RMS_EPS) * nw
            o_ref[rows, cols[hh]] = (on * (zf * jax.nn.sigmoid(zf))).astype(BF16)
        return carry

    lax.fori_loop(0, n_chunks, chunk_body, 0)


def deltanet(qkv, p1, gates_col, gates_row, a_log, dt_bias, norm_w, layer, dn_width, bs=512):
    s = qkv.shape[0]
    hb = DN_HEADS_PER_STEP
    c_ = DN_CHUNK
    heads = dn_width // DN_HEAD_DIM
    assert heads % hb == 0
    groups = heads // hb
    bs = min(bs, s)
    assert s % bs == 0 and bs % c_ == 0
    wblk = hb * DN_HEAD_DIM
    ncb = dn_width // wblk
    n_chunks = bs // c_
    a_log_c = a_log.reshape(-1, groups, 1, hb)
    a_log_r = a_log.reshape(-1, groups, hb, 1)
    dt_c = dt_bias.reshape(-1, groups, 1, hb)
    dt_r = dt_bias.reshape(-1, groups, hb, 1)
    nw = norm_w.reshape(-1, 1, DN_HEAD_DIM)

    def seg(which):
        return pl.BlockSpec((bs, wblk), lambda g, i: (i, which * ncb + g))

    col_spec = lambda w: pl.BlockSpec((None, None, bs, hb), lambda g, i: (w, g, i, 0))
    row_spec = lambda w: pl.BlockSpec((None, None, n_chunks, hb, c_), lambda g, i: (w, g, i, 0, 0))
    pc_spec = pl.BlockSpec((None, None, 1, hb), lambda g, i: (layer, g, 0, 0))
    pr_spec = pl.BlockSpec((None, None, hb, 1), lambda g, i: (layer, g, 0, 0))
    vmem = 2 * bs * wblk * (2 * 3 + 4 + 2) + hb * DN_HEAD_DIM * DN_HEAD_DIM * 4 + 16 * bs * V7X_LANES * 4
    return pl.pallas_call(
        functools.partial(_dn_kernel, hb=hb, chunk=c_, n_chunks=n_chunks),
        out_shape=jax.ShapeDtypeStruct((s, dn_width), BF16),
        grid=(groups, s // bs),
        in_specs=[seg(0), seg(1), seg(2), seg(3),
                  col_spec(0), col_spec(1), row_spec(0), row_spec(1),
                  pc_spec, pc_spec, pr_spec, pr_spec,
                  pl.BlockSpec((None, 1, DN_HEAD_DIM), lambda g, i: (layer, 0, 0))],
        out_specs=pl.BlockSpec((bs, wblk), lambda g, i: (i, g)),
        scratch_shapes=[pltpu.VMEM((hb, DN_HEAD_DIM, DN_HEAD_DIM), F32)],
        compiler_params=_params(("parallel", "arbitrary"), vmem),
        name="deltanet",
    )(qkv, qkv, qkv, p1, gates_col, gates_col, gates_row, gates_row, a_log_c, dt_c, a_log_r, dt_r, nw)


def _da_prep_kernel(q_ref, k_ref, v_ref, cos_ref, sin_ref, qo_ref, ko_ref, vo_ref, *, q_scale):
    c = cos_ref[...]
    s = sin_ref[...]
    half = DA_HEAD_DIM // 2
    for j in range(q_ref.shape[1] // DA_HEAD_DIM):
        sl = slice(j * DA_HEAD_DIM, (j + 1) * DA_HEAD_DIM)
        xq = q_ref[:, sl]
        xk = k_ref[:, sl]
        qo_ref[:, sl] = ((xq * c + pltpu.roll(xq, half, 1) * s) * q_scale).astype(BF16)
        ko_ref[:, sl] = (xk * c + pltpu.roll(xk, half, 1) * s).astype(BF16)
    vo_ref[...] = v_ref[...].astype(BF16)


def da_prep(p2, cos, sin_signed, da_width, bt=256):
    s = p2.shape[0]
    bt = min(bt, s)
    seg = lambda w: pl.BlockSpec((bt, da_width), lambda i: (i, w))
    tab = pl.BlockSpec((bt, DA_HEAD_DIM), lambda i: (i, 0))
    out = jax.ShapeDtypeStruct((s, da_width), BF16)
    return pl.pallas_call(
        functools.partial(_da_prep_kernel, q_scale=DA_HEAD_DIM ** -0.5 * LOG2E),
        out_shape=(out, out, out),
        grid=(s // bt,),
        in_specs=[seg(0), seg(1), seg(2), tab, tab],
        out_specs=(seg(0), seg(0), seg(0)),
        compiler_params=_params(("parallel",), 2 * 3 * bt * da_width * 6),
        name="da_prep",
    )(p2, p2, p2, cos, sin_signed)


def _da_kernel(lam_init_ref, q_ref, k_ref, v_ref, lq1_ref, lk1_ref, lq2_ref, lk2_ref, nw_ref, o_ref,
               acc_ref, m_ref, l_ref, s_a, s_b, mx_a, mx_b, *, blk):
    iq = pl.program_id(1)
    d_ = DA_HEAD_DIM
    rc = DA_ROW_CHUNK
    contract_last = (((1,), (1,)), ((), ()))
    m_ref[...] = jnp.full_like(m_ref, -jnp.inf)
    l_ref[...] = jnp.zeros_like(l_ref)
    acc_ref[...] = jnp.zeros_like(acc_ref)
    chains = [(c, sub) for c in range(blk // rc) for sub in range(2)]
    rsl = lambda c: slice(c * rc, (c + 1) * rc)
    lanes = lambda x, width: jnp.concatenate([x] * (width // V7X_LANES), axis=1)

    def scores(j, out):
        s_out, mx_out = out
        kb = k_ref[pl.ds(pl.multiple_of(j * blk, blk), blk), :]
        for sub in range(2):
            s_big = lax.dot_general(q_ref[:, sub * d_:(sub + 1) * d_], kb[:, sub * d_:(sub + 1) * d_],
                                    contract_last, preferred_element_type=F32)
            s_out[sub] = s_big
            mx_out[sub] = jnp.broadcast_to(jnp.max(s_big, -1, keepdims=True), (blk, V7X_LANES))

    def kv_step(j, cur, nxt, masked):
        if nxt is not None:
            scores(j + 1, nxt)
        s_cur, mx_cur = cur
        vb = v_ref[pl.ds(pl.multiple_of(j * blk, blk), blk), :]
        s = [s_cur[sub, rsl(c), :] for c, sub in chains]
        if masked:
            row = lax.broadcasted_iota(I32, (rc, blk), 0)
            col = lax.broadcasted_iota(I32, (rc, blk), 1)
            s = [jnp.where(col <= row + c * rc, s[n], -jnp.inf) for n, (c, sub) in enumerate(chains)]
            m_cur = [jnp.max(s[n], -1, keepdims=True) for n in range(len(chains))]
        else:
            m_cur = [mx_cur[sub, rsl(c), :] for c, sub in chains]
        alpha, pb = [], []
        for n, (c, sub) in enumerate(chains):
            m_old = m_ref[sub, rsl(c), :]
            m_new = jnp.maximum(m_old, m_cur[n])
            a_n = jnp.exp2(m_old - m_new)
            p = jnp.exp2(s[n] - lanes(m_new, blk))
            m_ref[sub, rsl(c), :] = m_new
            l_ref[sub, rsl(c), :] = a_n * l_ref[sub, rsl(c), :] + jnp.sum(p, -1, keepdims=True)
            alpha.append(a_n)
            pb.append(p.astype(BF16))
        for sub in range(2):
            mine = [n for n, (c, sb) in enumerate(chains) if sb == sub]
            p_sub = jnp.concatenate([pb[n] for n in mine], axis=0)
            a_sub = jnp.concatenate([alpha[n] for n in mine], axis=0)
            acc_ref[sub] = lanes(a_sub, DA_V_DIM) * acc_ref[sub] + _dot32(p_sub, vb)

    odd = lax.rem(iq, 2)
    buf_a, buf_b = (s_a, mx_a), (s_b, mx_b)

    @pl.when(odd == 0)
    def _even_start():
        scores(0, buf_a)

    @pl.when(odd == 1)
    def _odd_start():
        scores(0, buf_b)
        kv_step(0, buf_b, buf_a, False)

    def pair(jj, carry):
        j = odd + 2 * jj
        kv_step(j, buf_a, buf_b, False)
        kv_step(j + 1, buf_b, buf_a, False)
        return carry

    lax.fori_loop(0, (iq - odd) // 2, pair, 0)
    kv_step(iq, buf_a, None, True)

    lam_init = lam_init_ref[0]
    lam = (jnp.exp(jnp.sum(lq1_ref[...] * lk1_ref[...], keepdims=True))
           - jnp.exp(jnp.sum(lq2_ref[...] * lk2_ref[...], keepdims=True)) + lam_init)
    o = acc_ref[0] / lanes(l_ref[0], DA_V_DIM) - lam * (acc_ref[1] / lanes(l_ref[1], DA_V_DIM))
    on = o * lax.rsqrt(jnp.mean(o * o, -1, keepdims=True) + RMS_EPS) * nw_ref[...]
    o_ref[...] = (on * (1.0 - lam_init)).astype(BF16)


def diff_attention(qr, kr, vb, lq1, lk1, lq2, lk2, norm_w, lam_init, layer, blk=512):
    s, width = qr.shape
    heads = width // DA_V_DIM
    blk = min(blk, s)
    assert s % blk == 0
    vec = pl.BlockSpec((None, 1, DA_HEAD_DIM), lambda h, i, *_: (layer, 0, 0))
    full = pl.BlockSpec((s, DA_V_DIM), lambda h, i, *_: (0, h))
    tile = pl.BlockSpec((blk, DA_V_DIM), lambda h, i, *_: (i, h))
    grid_spec = pltpu.PrefetchScalarGridSpec(
        num_scalar_prefetch=1,
        grid=(heads, s // blk),
        in_specs=[tile, full, full, vec, vec, vec, vec,
                  pl.BlockSpec((None, 1, DA_V_DIM), lambda h, i, *_: (layer, 0, 0))],
        out_specs=tile,
        scratch_shapes=[pltpu.VMEM((2, blk, DA_V_DIM), F32), pltpu.VMEM((2, blk, V7X_LANES), F32),
                        pltpu.VMEM((2, blk, V7X_LANES), F32),
                        pltpu.VMEM((2, blk, blk), F32), pltpu.VMEM((2, blk, blk), F32),
                        pltpu.VMEM((2, blk, V7X_LANES), F32), pltpu.VMEM((2, blk, V7X_LANES), F32)],
    )
    vmem = 4 * s * DA_V_DIM * 2 + 4 * blk * DA_V_DIM * 2 + 2 * blk * DA_V_DIM * 4 + 8 * blk * blk * 4
    return pl.pallas_call(
        functools.partial(_da_kernel, blk=blk),
        out_shape=jax.ShapeDtypeStruct((s, width), BF16),
        grid_spec=grid_spec,
        compiler_params=_params(("parallel", "arbitrary"), vmem),
        name="diff_attention",
    )(lam_init, qr, kr, vb, lq1, lk1, lq2, lk2, norm_w)


def _mem_attn_kernel(q_ref, k_ref, v_ref, o_ref, *, scale):
    q = (q_ref[...] * scale).astype(BF16)
    k = k_ref[...].astype(BF16)
    sc = lax.dot_general(q, k, (((1,), (1,)), ((), ())), preferred_element_type=F32)
    sc = sc - jnp.max(sc, -1, keepdims=True)
    p = jnp.exp(sc)
    denom = jnp.sum(p, -1, keepdims=True)
    o = jnp.dot(p.astype(BF16), v_ref[...].astype(BF16), preferred_element_type=F32) / denom
    o_ref[...] = o.astype(BF16)


def memory_attention(p2, mem_kv, q_col0, mem_width, bt=512):
    s = p2.shape[0]
    n_mem = mem_kv.shape[0]
    hd = mem_width // MEM_HEADS
    bt = min(bt, s)
    assert q_col0 % hd == 0
    qb0 = q_col0 // hd
    return pl.pallas_call(
        functools.partial(_mem_attn_kernel, scale=hd ** -0.5),
        out_shape=jax.ShapeDtypeStruct((s, mem_width), BF16),
        grid=(s // bt, MEM_HEADS),
        in_specs=[pl.BlockSpec((bt, hd), lambda i, h: (i, qb0 + h)),
                  pl.BlockSpec((n_mem, hd), lambda i, h: (0, h)),
                  pl.BlockSpec((n_mem, hd), lambda i, h: (0, MEM_HEADS + h))],
        out_specs=pl.BlockSpec((bt, hd), lambda i, h: (i, h)),
        compiler_params=_params(("parallel", "parallel"), 4 * bt * hd * 4 + 4 * n_mem * hd * 4 + 4 * bt * n_mem * 4),
        name="memory_attention",
    )(p2, mem_kv, mem_kv)


def _router_kernel(x_ref, w_ref, b_ref, idx_ref, gate_ref, rank_ref, cnt_ref, carry_ref):
    i = pl.program_id(0)

    @pl.when(i == 0)
    def _reset():
        carry_ref[...] = jnp.zeros_like(carry_ref)

    x = x_ref[...]
    w = w_ref[...]
    x_hi = x.astype(BF16)
    w_hi = w.astype(BF16)
    x_lo = (x - x_hi.astype(F32)).astype(BF16)
    w_lo = (w - w_hi.astype(F32)).astype(BF16)
    logits = _dot32(x_hi, w_hi) + (_dot32(x_lo, w_hi) + _dot32(x_hi, w_lo)) + b_ref[...]
    bt, n_e = logits.shape
    lane = lax.broadcasted_iota(I32, (bt, n_e), 1)
    work = logits
    vals, sels = [], []
    for kk in range(TOP_K):
        mx = jnp.max(work, -1, keepdims=True)
        idx = jnp.min(jnp.where(work == mx, lane, n_e), -1, keepdims=True)
        sel = lane == idx
        vals.append(mx)
        sels.append(sel)
        idx_ref[:, kk:kk + 1] = idx
        work = jnp.where(sel, -jnp.inf, work)
    exps = [jnp.exp(v - vals[0]) for v in vals]
    denom = exps[0]
    for e in exps[1:]:
        denom = denom + e
    for kk in range(TOP_K):
        gate_ref[:, kk:kk + 1] = exps[kk] / denom
    mask = sels[0]
    for sel in sels[1:]:
        mask = mask | sel
    maskf = mask.astype(F32)
    row = lax.broadcasted_iota(I32, (bt, bt), 0)
    col = lax.broadcasted_iota(I32, (bt, bt), 1)
    before = (row > col).astype(BF16)
    rank = carry_ref[...] + jnp.dot(before, maskf.astype(BF16), preferred_element_type=F32)
    for kk in range(TOP_K):
        rank_ref[:, kk:kk + 1] = jnp.sum(jnp.where(sels[kk], rank, 0.0), -1, keepdims=True).astype(I32)
    carry_ref[...] = carry_ref[...] + jnp.sum(maskf, 0, keepdims=True)
    cnt_ref[...] = carry_ref[...].astype(I32)


def router(x, router_w, router_b, layer, bt=256):
    s, d = x.shape
    n_e = router_w.shape[-1]
    bt = min(bt, s)
    small = lambda dt: jax.ShapeDtypeStruct((s, TOP_K), dt)
    tok = pl.BlockSpec((bt, TOP_K), lambda i: (i, 0))
    return pl.pallas_call(
        _router_kernel,
        out_shape=(small(I32), small(F32), small(I32), jax.ShapeDtypeStruct((1, n_e), I32)),
        grid=(s // bt,),
        in_specs=[pl.BlockSpec((bt, d), lambda i: (i, 0)),
                  pl.BlockSpec((None, d, n_e), lambda i: (layer, 0, 0)),
                  pl.BlockSpec((None, 1, n_e), lambda i: (layer, 0, 0))],
        out_specs=(tok, tok, tok, pl.BlockSpec((1, n_e), lambda i: (0, 0))),
        scratch_shapes=[pltpu.VMEM((1, n_e), F32)],
        compiler_params=_params(("arbitrary",), 2 * bt * d * 4 + 2 * d * V7X_LANES * 4),
        name="router",
    )(x, router_w, router_b)


def _row_copy(src, src_row, dst, dst_row, sem):
    return pltpu.make_async_copy(src.at[pl.ds(src_row, 1)], dst.at[pl.ds(dst_row, 1)], sem)


def _moe_kernel(te_ref, nt_ref, src_next_ref, dst_prev_ref, dst_ref, src_ref, x_hbm, wg_ref, wu_ref, wd_ref,
                bg_ref, bu_ref, bd_ref, y_hbm, xbuf, ybuf, wgb, wub, wdb, gsem, ssem, *, tm):
    i = pl.program_id(0)
    nt = nt_ref[0]
    slot = lax.rem(i, 2)
    other = 1 - slot
    groups = MOE_DMA_GROUPS
    bounds = [round(g * tm / groups) for g in range(groups + 1)]

    def start_gather(idx_ref, s_, g):
        for r in range(bounds[g], bounds[g + 1]):
            _row_copy(x_hbm, idx_ref[0, r], xbuf.at[s_], r, gsem.at[s_]).start()

    def start_scatter(idx_ref, s_, g):
        for r in range(bounds[g], bounds[g + 1]):
            _row_copy(ybuf.at[s_], r, y_hbm, idx_ref[0, r], ssem.at[s_]).start()

    def wait_gather(s_):
        pltpu.make_async_copy(x_hbm.at[pl.ds(0, tm)], xbuf.at[s_], gsem.at[s_]).wait()

    def wait_scatter(s_):
        pltpu.make_async_copy(ybuf.at[s_], y_hbm.at[pl.ds(0, tm)], ssem.at[s_]).wait()

    @pl.when(i == 0)
    def _prologue():
        ybuf[other] = jnp.zeros((tm, ybuf.shape[2]), U32)
        spare = pltpu.make_async_copy(ybuf.at[other], y_hbm.at[pl.ds(y_hbm.shape[0] - tm, tm)], ssem.at[other])
        spare.start()
        spare.wait()
        for g in range(groups):
            start_gather(src_ref, slot, g)

    @pl.when(i < nt)
    def _tile():
        wait_gather(slot)
        prev = te_ref[jnp.maximum(i - 1, 0)]

        @pl.when((i == 0) | (te_ref[i] != prev))
        def _cast_weights():
            wgb[...] = wg_ref[...].astype(BF16)
            wub[...] = wu_ref[...].astype(BF16)
            wdb[...] = wd_ref[...].astype(BF16)

        half = xbuf.shape[2]
        x_lo, x_hi = _unpack_bf16_pair(xbuf[slot])
        x_lo, x_hi = x_lo.astype(BF16), x_hi.astype(BF16)
        gate = _dot32(x_lo, wgb[:half, :])
        start_gather(src_next_ref, other, 0)
        up = _dot32(x_lo, wub[:half, :])
        start_gather(src_next_ref, other, 1)
        gate = gate + _dot32(x_hi, wgb[half:, :]) + bg_ref[...]
        start_gather(src_next_ref, other, 2)
        up = up + _dot32(x_hi, wub[half:, :]) + bu_ref[...]
        start_scatter(dst_prev_ref, other, 0)
        gate = jnp.minimum(gate, SWIGLU_LIMIT)
        up = jnp.clip(up, -SWIGLU_LIMIT, SWIGLU_LIMIT)
        act = ((up + 1.0) * gate * jax.nn.sigmoid(SWIGLU_ALPHA * gate)).astype(BF16)
        y_lo = _dot32(act, wdb[:, :half]) + bd_ref[:, :half]
        start_scatter(dst_prev_ref, other, 1)
        y_hi = _dot32(act, wdb[:, half:]) + bd_ref[:, half:]
        start_scatter(dst_prev_ref, other, 2)
        ybuf[slot] = _pack_bf16_pair(y_lo, y_hi)
        wait_scatter(other)

        @pl.when(i == nt - 1)
        def _drain_last():
            wait_gather(other)
            for g in range(groups):
                start_scatter(dst_ref, slot, g)
            wait_scatter(slot)


def routed_experts(x, tile_expert, n_tiles, src_tok, dst_row, w_gate, b_gate, w_up, b_up, w_down, b_down, layer):
    s, half = x.shape
    d = 2 * half
    f = w_gate.shape[-1]
    tm = MOE_TILE
    nt = tile_expert.shape[0]
    idx_spec = pl.BlockSpec((None, 1, tm), lambda i, *_: (i, 0, 0), memory_space=pltpu.SMEM)
    idx_next = pl.BlockSpec((None, 1, tm), lambda i, *_: (jnp.minimum(i + 1, nt - 1), 0, 0),
                            memory_space=pltpu.SMEM)
    idx_prev = pl.BlockSpec((None, 1, tm), lambda i, *_: (jnp.maximum(i - 1, 0), 0, 0), memory_space=pltpu.SMEM)
    wspec = lambda a, b: pl.BlockSpec((None, None, a, b), lambda i, te, n: (layer, te[i], 0, 0))
    grid_spec = pltpu.PrefetchScalarGridSpec(
        num_scalar_prefetch=2,
        grid=(nt,),
        in_specs=[idx_next, idx_prev, idx_spec, idx_spec, pl.BlockSpec(memory_space=pl.ANY),
                  wspec(d, f), wspec(d, f), wspec(f, d), wspec(1, f), wspec(1, f), wspec(1, d)],
        out_specs=pl.BlockSpec(memory_space=pl.ANY),
        scratch_shapes=[pltpu.VMEM((2, tm, half), U32), pltpu.VMEM((2, tm, half), U32),
                        pltpu.VMEM((d, f), BF16), pltpu.VMEM((d, f), BF16), pltpu.VMEM((f, d), BF16),
                        pltpu.SemaphoreType.DMA((2,)), pltpu.SemaphoreType.DMA((2,))],
    )
    vmem = 2 * 3 * d * f * 4 + 3 * d * f * 2 + 4 * tm * half * 4 + 4 * tm * d * 4
    return pl.pallas_call(
        functools.partial(_moe_kernel, tm=tm),
        out_shape=jax.ShapeDtypeStruct((TOP_K * s + tm, half), U32),
        grid_spec=grid_spec,
        compiler_params=_params(("arbitrary",), vmem),
        name="routed_experts",
    )(tile_expert, n_tiles, src_tok, dst_row, dst_row, src_tok, x, w_gate, w_up, w_down, b_gate, b_up, b_down)


def _combine_ln_kernel(x_ref, y0_ref, y1_ref, y2_ref, y3_ref, gate_ref, g_ref, b_ref, o_ref, ob_ref, *, alpha):
    half = y0_ref.shape[1]
    gates = gate_ref[...]
    ffn_lo = ffn_hi = None
    for kk, y_ref in enumerate((y0_ref, y1_ref, y2_ref, y3_ref)):
        y_lo, y_hi = _unpack_bf16_pair(y_ref[...])
        gk = gates[:, kk:kk + 1]
        ffn_lo = gk * y_lo if ffn_lo is None else ffn_lo + gk * y_lo
        ffn_hi = gk * y_hi if ffn_hi is None else ffn_hi + gk * y_hi
    o_lo, o_hi = _layer_norm_halves(alpha * x_ref[:, :half] + ffn_lo, alpha * x_ref[:, half:] + ffn_hi,
                                    g_ref[...], b_ref[...])
    o_ref[:, :half] = o_lo
    o_ref[:, half:] = o_hi
    ob_ref[:, :half] = o_lo.astype(BF16)
    ob_ref[:, half:] = o_hi.astype(BF16)


def combine_layer_norm(x, y, gates, g, b, layer, alpha, bt=256):
    s, d = x.shape
    half = d // 2
    bt = min(bt, s)
    nb = s // bt
    row = pl.BlockSpec((bt, d), lambda i: (i, 0))
    yk = lambda kk: pl.BlockSpec((bt, half), lambda i: (kk * nb + i, 0))
    par = pl.BlockSpec((None, 1, d), lambda i: (layer, 0, 0))
    return pl.pallas_call(
        functools.partial(_combine_ln_kernel, alpha=alpha),
        out_shape=(jax.ShapeDtypeStruct((s, d), F32), jax.ShapeDtypeStruct((s, d), BF16)),
        grid=(nb,),
        in_specs=[row, yk(0), yk(1), yk(2), yk(3), pl.BlockSpec((bt, TOP_K), lambda i: (i, 0)), par, par],
        out_specs=(row, row),
        compiler_params=_params(("parallel",), 2 * bt * d * (4 + 4 * 2 + 4 + 2)),
        name="combine_ln",
    )(x, y, y, y, y, gates, g, b)


def _routing_plan(idx, rank, counts, s, n_e):
    tm = MOE_TILE
    nt = s * TOP_K // tm + n_e
    counts = counts.reshape(n_e)
    tiles_e = (counts + tm - 1) // tm
    tile_end = jnp.cumsum(tiles_e)
    tile_start = tile_end - tiles_e
    n_tiles = tile_end[-1]
    tile_ids = jnp.arange(nt, dtype=I32)
    te = jnp.sum((tile_end[None, :] <= jnp.minimum(tile_ids, n_tiles - 1)[:, None]).astype(I32), axis=1)
    te = jnp.clip(te, 0, n_e - 1).astype(I32)
    pos = (tile_start * tm)[idx] + rank
    tok = jnp.arange(s, dtype=I32)[:, None]
    val = jnp.arange(TOP_K, dtype=I32)[None, :] * s + tok
    spare = jnp.tile(TOP_K * s + jnp.arange(tm, dtype=I32), nt)
    slots = spare.at[pos.reshape(-1)].set(val.reshape(-1), unique_indices=True)
    dst_row = slots.reshape(nt, 1, tm)
    src_tok = (slots % s).reshape(nt, 1, tm)
    return te, n_tiles.reshape(1).astype(I32), src_tok, dst_row


def kernel(x, mem, positions, w_in, conv_w, a_log, dt_bias, dn_norm_w, lambda_q1, lambda_k1, lambda_q2,
           lambda_k2, da_norm_w, w_mem_kv, w_o, ln1_g, ln1_b, router_w, router_b, w_gate, b_gate, w_up, b_up,
           w_down, b_down, ln2_g, ln2_b):
    bsz, s, d = x.shape
    assert bsz == 1
    depth = w_in.shape[0]
    n_e = router_w.shape[-1]
    f = w_gate.shape[-1]
    dn_w = 3 * d // 8
    da_w = 3 * d // 8
    mem_w = d - dn_w - da_w
    dn_heads = dn_w // DN_HEAD_DIM
    alpha = (2 * depth) ** 0.25
    hb = DN_HEADS_PER_STEP
    groups = dn_heads // hb
    c_ = DN_CHUNK

    inv = 1.0 / (ROPE_THETA ** (jnp.arange(0, DA_HEAD_DIM, 2, dtype=F32) / DA_HEAD_DIM))
    ang = positions[0].astype(F32)[:, None] * inv
    ang = jnp.concatenate([ang, ang], -1)
    sign = jnp.concatenate([-jnp.ones((DA_HEAD_DIM // 2,), F32), jnp.ones((DA_HEAD_DIM // 2,), F32)])
    cos, sin_signed = jnp.cos(ang), jnp.sin(ang) * sign
    gate0 = 4 * dn_w
    post0 = gate0 + 2 * dn_heads
    w_in_t = jnp.swapaxes(w_in, 1, 2)
    lam_inits = [0.8 - 0.6 * math.exp(-0.3 * layer) for layer in range(depth)]
    vec3 = lambda p: p.reshape(depth, 1, -1)
    ln1_g3, ln1_b3, ln2_g3, ln2_b3 = vec3(ln1_g), vec3(ln1_b), vec3(ln2_g), vec3(ln2_b)
    lq1, lk1, lq2, lk2 = vec3(lambda_q1), vec3(lambda_k1), vec3(lambda_q2), vec3(lambda_k2)
    da_nw = vec3(da_norm_w)
    router_b3 = vec3(router_b)
    b_gate4 = b_gate.reshape(depth, n_e, 1, f)
    b_up4 = b_up.reshape(depth, n_e, 1, f)
    b_down4 = b_down.reshape(depth, n_e, 1, d)

    xf = x[0]
    xb = cast_bf16(xf)
    memb = cast_bf16(mem[0])
    for layer in range(depth):
        p1 = matmul_nt(xb, w_in_t, layer, 4 * dn_w, name="proj_dn")
        p2 = matmul_nt(xb, w_in_t, layer, 3 * da_w + mem_w, row0=post0, name="proj_attn")
        gts = matmul_nt(xb, w_in_t, layer, V7X_LANES, row0=gate0, bn=V7X_LANES, name="proj_gates")
        mem_kv = matmul([memb], w_mem_kv, layer, 2 * mem_w, name="proj_mem")

        ab = gts[:, :2 * dn_heads].reshape(s, 2, groups, hb)
        gates_col = ab.transpose(1, 2, 0, 3)
        gates_row = ab.reshape(s // c_, c_, 2, groups, hb).transpose(2, 3, 0, 4, 1)
        qkv = dn_prep(p1, conv_w, layer, dn_w)
        y_dn = deltanet(qkv, p1, gates_col, gates_row, a_log, dt_bias, dn_norm_w, layer, dn_w)

        qr, kr, vb = da_prep(p2, cos, sin_signed, da_w)
        lam_init = jnp.full((1,), lam_inits[layer], F32)
        y_da = diff_attention(qr, kr, vb, lq1, lk1, lq2, lk2, da_nw, lam_init, layer)
        y_mem = memory_attention(p2, mem_kv, 3 * da_w, mem_w)

        mixed = matmul([y_dn, y_da, y_mem], w_o, layer, d, name="proj_out")
        x1, x1_packed = residual_layer_norm(xf, mixed, ln1_g3, ln1_b3, layer, alpha)

        idx, gates, rank, counts = router(x1, router_w, router_b3, layer)
        te, n_tiles, src_tok, dst_row = _routing_plan(idx, rank, counts, s, n_e)
        y = routed_experts(x1_packed, te, n_tiles, src_tok, dst_row, w_gate, b_gate4, w_up, b_up4, w_down, b_down4,
                           layer)
        xf, xb = combine_layer_norm(x1, y, gates, ln2_g3, ln2_b3, layer, alpha)
    return xf[None]
```

```python
import functools
import math

import jax
import jax.numpy as jnp
from jax import lax
from jax.experimental import pallas as pl
from jax.experimental.pallas import tpu as pltpu

F32 = jnp.float32
BF16 = jnp.bfloat16
I32 = jnp.int32
U32 = jnp.uint32
HIGHEST = lax.Precision.HIGHEST
HIGH_HALF = 0xFFFF0000

V7X_LANES = 128
V7X_SUBLANES = 8
V7X_VMEM_BYTES = 64 * 1024 * 1024
VMEM_COMPILER_MARGIN = 6 * 1024 * 1024

DN_HEAD_DIM = 128
CONV_K = 4
DA_HEAD_DIM = 128
DA_V_DIM = 2 * DA_HEAD_DIM
MEM_HEADS = 4
TOP_K = 4
ROPE_THETA = 10000.0
SWIGLU_LIMIT = 7.0
SWIGLU_ALPHA = 1.702
LN_EPS = 1e-5
RMS_EPS = 1e-6
L2_EPS = 1e-6
LOG2E = 1.4426950408889634

DN_CHUNK = 128
DN_HEADS_PER_STEP = 12
DN_INV_BASE = 16
MOE_TILE = 256
MOE_DMA_GROUPS = 3
MOE_SPARE_REGIONS = 3
MM_CAST_COLS = 512
DA_ROW_CHUNK = 128


def _vmem_limit(nbytes):
    return int(min(nbytes + VMEM_COMPILER_MARGIN, V7X_VMEM_BYTES - 2 * 1024 * 1024))


def _params(semantics, vmem_bytes):
    return pltpu.CompilerParams(dimension_semantics=semantics, vmem_limit_bytes=_vmem_limit(vmem_bytes))


def _cast_kernel(x_ref, o_ref):
    o_ref[...] = x_ref[...].astype(o_ref.dtype)


def cast_bf16(x, bt=256):
    m, d = x.shape
    bt = min(bt, m)
    return pl.pallas_call(
        _cast_kernel,
        out_shape=jax.ShapeDtypeStruct((m, d), BF16),
        grid=(m // bt,),
        in_specs=[pl.BlockSpec((bt, d), lambda i: (i, 0))],
        out_specs=pl.BlockSpec((bt, d), lambda i: (i, 0)),
        compiler_params=_params(("parallel",), 2 * bt * d * 6),
        name="cast_bf16",
    )(x)


def _mm_kernel(*refs, k_splits):
    n_a = len(k_splits)
    a_refs, w_ref, o_ref, wb_ref = refs[:n_a], refs[n_a], refs[n_a + 1], refs[n_a + 2]

    @pl.when(pl.program_id(1) == 0)
    def _cast_weights():
        wb_ref[...] = w_ref[...].astype(BF16)

    acc = None
    k0 = 0
    for a_ref, kk in zip(a_refs, k_splits):
        part = jnp.dot(a_ref[...], wb_ref[k0:k0 + kk, :], preferred_element_type=F32)
        acc = part if acc is None else acc + part
        k0 += kk
    o_ref[...] = acc.astype(o_ref.dtype)


def matmul(a_list, w, layer, ncols, *, bm=1024, bn=512, out_dtype=F32, name="proj"):
    m = a_list[0].shape[0]
    k_splits = tuple(a.shape[1] for a in a_list)
    k = sum(k_splits)
    assert w.shape[1] == k and ncols % bn == 0
    bm = min(bm, m)
    assert m % bm == 0
    in_specs = [pl.BlockSpec((bm, kk), lambda j, i: (i, 0)) for kk in k_splits]
    in_specs.append(pl.BlockSpec((None, k, bn), lambda j, i: (layer, 0, j)))
    vmem = 2 * bm * k * 2 + 2 * k * bn * 4 + k * bn * 2 + 2 * bm * bn * 4
    return pl.pallas_call(
        functools.partial(_mm_kernel, k_splits=k_splits),
        out_shape=jax.ShapeDtypeStruct((m, ncols), out_dtype),
        grid=(ncols // bn, m // bm),
        in_specs=in_specs,
        out_specs=pl.BlockSpec((bm, bn), lambda j, i: (i, j)),
        scratch_shapes=[pltpu.VMEM((k, bn), BF16)],
        compiler_params=_params(("arbitrary", "arbitrary"), vmem),
        name=name,
    )(*a_list, w)


def _mm_nt_kernel(*refs, row_shift, rope_tiles, rope_scaled_tiles, rope_scale):
    refs = list(refs)
    a_ref, w_ref = refs[:2]
    w_tail_ref = refs[2] if row_shift else None
    cos_ref, sin_ref = (refs[-4], refs[-3]) if rope_tiles else (None, None)
    o_ref, wb_ref = refs[-2:]

    @pl.when(pl.program_id(1) == 0)
    def _cast_weights():
        bn, k = wb_ref.shape
        if not row_shift:
            wb_ref[...] = w_ref[...].astype(BF16)
        else:
            for c in range(k // MM_CAST_COLS):
                cols = slice(c * MM_CAST_COLS, (c + 1) * MM_CAST_COLS)
                both = jnp.concatenate([w_ref[:, cols], w_tail_ref[:, cols]], axis=0)
                wb_ref[:, cols] = both[row_shift:row_shift + bn].astype(BF16)

    acc = lax.dot_general(a_ref[...], wb_ref[...], (((1,), (1,)), ((), ())), preferred_element_type=F32)
    if not rope_tiles:
        o_ref[...] = acc.astype(o_ref.dtype)
    else:
        j = pl.program_id(0)

        @pl.when(j < rope_tiles)
        def _rotary():
            c = cos_ref[...]
            s = sin_ref[...]
            scale = jnp.where(j < rope_scaled_tiles, rope_scale, 1.0).astype(F32)
            for t in range(acc.shape[1] // DA_HEAD_DIM):
                sl = slice(t * DA_HEAD_DIM, (t + 1) * DA_HEAD_DIM)
                xh = acc[:, sl]
                o_ref[:, sl] = ((xh * c + pltpu.roll(xh, DA_HEAD_DIM // 2, 1) * s) * scale).astype(o_ref.dtype)

        @pl.when(j >= rope_tiles)
        def _plain():
            o_ref[...] = acc.astype(o_ref.dtype)


def matmul_nt(a, w_t, layer, ncols, *, row0=0, bm=1024, bn=512, out_dtype=F32, rope=None, name="proj_t"):
    m, k = a.shape
    assert w_t.shape[2] == k and ncols % bn == 0 and k % MM_CAST_COLS == 0
    bm = min(bm, m)
    assert m % bm == 0
    row_block0, row_shift = divmod(row0, bn)
    assert row_shift < V7X_LANES and row_shift % V7X_SUBLANES == 0
    in_specs = [pl.BlockSpec((bm, k), lambda j, i: (i, 0)),
                pl.BlockSpec((None, bn, k), lambda j, i: (layer, j + row_block0, 0))]
    operands = [a, w_t]
    if row_shift:
        per = bn // V7X_LANES
        in_specs.append(pl.BlockSpec((None, V7X_LANES, k), lambda j, i: (layer, (j + row_block0 + 1) * per, 0)))
        operands.append(w_t)
    rope_tiles = rope_scaled_tiles = 0
    rope_scale = 1.0
    if rope is not None:
        cos, sin_signed, n_rope_cols, n_scaled_cols, rope_scale = rope
        assert n_rope_cols % bn == 0 and n_scaled_cols % bn == 0
        rope_tiles, rope_scaled_tiles = n_rope_cols // bn, n_scaled_cols // bn
        tab = pl.BlockSpec((bm, DA_HEAD_DIM), lambda j, i: (i, 0))
        in_specs += [tab, tab]
        operands += [cos, sin_signed]
    vmem = 2 * bm * k * 2 + 2 * k * (bn + V7X_LANES) * 4 + k * bn * 2 + 2 * bm * bn * 4 + 4 * bm * DA_HEAD_DIM * 4
    return pl.pallas_call(
        functools.partial(_mm_nt_kernel, row_shift=row_shift, rope_tiles=rope_tiles,
                          rope_scaled_tiles=rope_scaled_tiles, rope_scale=rope_scale),
        out_shape=jax.ShapeDtypeStruct((m, ncols), out_dtype),
        grid=(ncols // bn, m // bm),
        in_specs=in_specs,
        out_specs=pl.BlockSpec((bm, bn), lambda j, i: (i, j)),
        scratch_shapes=[pltpu.VMEM((bn, k), BF16)],
        compiler_params=_params(("arbitrary", "arbitrary"), vmem),
        name=name,
    )(*operands)


def _pack_bf16_pair(lo, hi):
    lo_bits = lax.bitcast_convert_type(lo.astype(BF16).astype(F32), U32)
    hi_bits = lax.bitcast_convert_type(hi.astype(BF16).astype(F32), U32)
    return (lo_bits >> 16) | (hi_bits & U32(HIGH_HALF))


def _unpack_bf16_pair(w):
    return lax.bitcast_convert_type(w << 16, F32), lax.bitcast_convert_type(w & U32(HIGH_HALF), F32)


def _layer_norm_halves(h_lo, h_hi, g, b):
    half = h_lo.shape[1]
    n = 2 * half
    mu = (jnp.sum(h_lo, -1, keepdims=True) + jnp.sum(h_hi, -1, keepdims=True)) / n
    d_lo, d_hi = h_lo - mu, h_hi - mu
    var = (jnp.sum(d_lo * d_lo, -1, keepdims=True) + jnp.sum(d_hi * d_hi, -1, keepdims=True)) / n
    inv = lax.rsqrt(var + LN_EPS)
    return d_lo * inv * g[:, :half] + b[:, :half], d_hi * inv * g[:, half:] + b[:, half:]


def _ln_kernel(x_ref, y_ref, g_ref, b_ref, o_ref, op_ref, *, alpha):
    half = op_ref.shape[1]
    o_lo, o_hi = _layer_norm_halves(alpha * x_ref[:, :half] + y_ref[:, :half],
                                    alpha * x_ref[:, half:] + y_ref[:, half:], g_ref[...], b_ref[...])
    o_ref[:, :half] = o_lo
    o_ref[:, half:] = o_hi
    op_ref[...] = _pack_bf16_pair(o_lo, o_hi)


def residual_layer_norm(x, y, g, b, layer, alpha, bt=256):
    s, d = x.shape
    bt = min(bt, s)
    row = pl.BlockSpec((bt, d), lambda i: (i, 0))
    par = pl.BlockSpec((None, 1, d), lambda i: (layer, 0, 0))
    return pl.pallas_call(
        functools.partial(_ln_kernel, alpha=alpha),
        out_shape=(jax.ShapeDtypeStruct((s, d), F32), jax.ShapeDtypeStruct((s, d // 2), U32)),
        grid=(s // bt,),
        in_specs=[row, row, par, par],
        out_specs=(row, pl.BlockSpec((bt, d // 2), lambda i: (i, 0))),
        compiler_params=_params(("parallel",), 2 * bt * d * (4 + 4 + 4 + 2)),
        name="residual_ln",
    )(x, y, g, b)


def _dn_prep_kernel(x_ref, halo_ref, w_ref, o_ref, xs_ref, *, bt, n_q_blocks, n_qk_blocks, q_scale):
    i = pl.program_id(0)
    j = pl.program_id(1)
    halo = halo_ref[...]
    xs_ref[0:V7X_SUBLANES, :] = jnp.where(i == 0, jnp.zeros_like(halo), halo)
    xs_ref[V7X_SUBLANES:, :] = x_ref[...]
    w = w_ref[...]
    y = None
    for tap in range(CONV_K):
        start = V7X_SUBLANES - (CONV_K - 1) + tap
        term = xs_ref[start:start + bt, :] * w[tap:tap + 1, :]
        y = term if y is None else y + term
    y = y * jax.nn.sigmoid(y)
    bc = y.shape[1]
    scale = jnp.where(j < n_q_blocks, q_scale, 1.0).astype(F32)
    for c in range(bc // DN_HEAD_DIM):
        sl = slice(c * DN_HEAD_DIM, (c + 1) * DN_HEAD_DIM)
        yh = y[:, sl]
        nrm = yh * (lax.rsqrt(jnp.sum(yh * yh, -1, keepdims=True) + L2_EPS) * scale)
        o_ref[:, sl] = jnp.where(j < n_qk_blocks, nrm, yh).astype(BF16)


def dn_prep(p1, conv_w, layer, dn_width, bt=512, bc=512):
    s = p1.shape[0]
    ncols = 3 * dn_width
    bt = min(bt, s)
    assert dn_width % bc == 0 and s % bt == 0
    hb = bt // V7X_SUBLANES
    return pl.pallas_call(
        functools.partial(_dn_prep_kernel, bt=bt, n_q_blocks=dn_width // bc, n_qk_blocks=2 * dn_width // bc,
                          q_scale=DN_HEAD_DIM ** -0.5),
        out_shape=jax.ShapeDtypeStruct((s, ncols), BF16),
        grid=(s // bt, ncols // bc),
        in_specs=[
            pl.BlockSpec((bt, bc), lambda i, j: (i, j)),
            pl.BlockSpec((V7X_SUBLANES, bc), lambda i, j: (jnp.maximum(i * hb - 1, 0), j)),
            pl.BlockSpec((None, CONV_K, bc), lambda i, j: (layer, 0, j)),
        ],
        out_specs=pl.BlockSpec((bt, bc), lambda i, j: (i, j)),
        scratch_shapes=[pltpu.VMEM((bt + V7X_SUBLANES, bc), F32)],
        compiler_params=_params(("parallel", "parallel"), 4 * bt * bc * 4),
        name="dn_prep",
    )(p1, p1, conv_w)


def _softplus(x):
    return jnp.maximum(x, 0.0) + jnp.log1p(jnp.exp(-jnp.abs(x)))


def _dot32(a, b):
    return jnp.dot(a, b, preferred_element_type=F32)


def _unit_lower_inverses(a_list, row, col):
    c = a_list[0].shape[0]
    hs = range(len(a_list))
    eye = (row == col).astype(F32)
    dist = row ^ col
    n = [jnp.where(dist < DN_INV_BASE, -a, 0.0) for a in a_list]
    t = [eye + n[h] for h in hs]
    p = [n[h].astype(BF16) for h in hs]
    for _ in range(int(math.log2(DN_INV_BASE)) - 1):
        p = [_dot32(p[h], p[h]).astype(BF16) for h in hs]
        t = [t[h] + _dot32(t[h].astype(BF16), p[h]) for h in hs]
    b = DN_INV_BASE
    while b < c:
        pair = (dist >= b) & (dist < 2 * b)
        mb = [jnp.where(pair, a, 0.0).astype(BF16) for a in a_list]
        tb = [t[h].astype(BF16) for h in hs]
        tm = [_dot32(tb[h], mb[h]).astype(BF16) for h in hs]
        t = [t[h] - _dot32(tm[h], tb[h]) for h in hs]
        b *= 2
    return t


def _dn_kernel(q_ref, k_ref, v_ref, z_ref, ac_ref, bc_ref, ar_ref, br_ref, alc_ref, dtc_ref, alr_ref, dtr_ref,
               nw_ref, o_ref, state_ref, *, hb, chunk, n_chunks):
    c_ = chunk
    d_ = DN_HEAD_DIM

    @pl.when(pl.program_id(1) == 0)
    def _reset():
        state_ref[...] = jnp.zeros_like(state_ref)

    row = lax.broadcasted_iota(I32, (c_, c_), 0)
    col = lax.broadcasted_iota(I32, (c_, c_), 1)
    incl = row >= col
    strict = row > col
    ltri = incl.astype(F32)
    neg_decay_c = -jnp.exp(alc_ref[...])
    neg_decay_r = -jnp.exp(alr_ref[...])
    nw = nw_ref[...]
    contract_last = (((1,), (1,)), ((), ()))
    contract_first = (((0,), (0,)), ((), ()))

    def chunk_body(ci, carry):
        r0 = pl.multiple_of(ci * c_, c_)
        rows = pl.ds(r0, c_)
        g_c = neg_decay_c * _softplus(ac_ref[rows, :] + dtc_ref[...])
        beta_c = jax.nn.sigmoid(bc_ref[rows, :])
        cum_c = jnp.dot(ltri, g_c, precision=HIGHEST, preferred_element_type=F32)
        g_r = neg_decay_r * _softplus(ar_ref[ci] + dtr_ref[...])
        cum_r = lax.dot_general(g_r, ltri, contract_last, precision=HIGHEST, preferred_element_type=F32)
        hs = range(hb)
        cols = [slice(hh * d_, (hh + 1) * d_) for hh in hs]
        gc = [cum_c[:, hh:hh + 1] for hh in hs]
        gr = [cum_r[hh:hh + 1, :] for hh in hs]
        bcol = [beta_c[:, hh:hh + 1] for hh in hs]
        qh = [q_ref[rows, cols[hh]] for hh in hs]
        kh = [k_ref[rows, cols[hh]] for hh in hs]
        vh = [v_ref[rows, cols[hh]] for hh in hs]
        kk = [lax.dot_general(kh[hh], kh[hh], contract_last, preferred_element_type=F32) for hh in hs]
        qk = [lax.dot_general(qh[hh], kh[hh], contract_last, preferred_element_type=F32) for hh in hs]
        decay = [jnp.exp(jnp.where(incl, gc[hh] - gr[hh], -jnp.inf)) for hh in hs]
        a = [jnp.where(strict, bcol[hh] * kk[hh] * decay[hh], 0.0) for hh in hs]
        t = _unit_lower_inverses(a, row, col)
        e_g = [jnp.exp(gc[hh]) for hh in hs]
        kf = [kh[hh].astype(F32) for hh in hs]
        rhs = [jnp.concatenate([vh[hh].astype(F32) * bcol[hh], kf[hh] * (bcol[hh] * e_g[hh])], axis=1).astype(BF16)
               for hh in hs]
        uw = [_dot32(t[hh].astype(BF16), rhs[hh]) for hh in hs]
        state = [state_ref[hh] for hh in hs]
        wq = [jnp.concatenate([uw[hh][:, d_:].astype(BF16), (qh[hh].astype(F32) * e_g[hh]).astype(BF16)], axis=0)
              for hh in hs]
        ws = [_dot32(wq[hh], state[hh].astype(BF16)) for hh in hs]
        v_new = [(uw[hh][:, :d_] - ws[hh][:c_]).astype(BF16) for hh in hs]
        o = [ws[hh][c_:] + _dot32((qk[hh] * decay[hh]).astype(BF16), v_new[hh]) for hh in hs]
        g_last = [gc[hh][c_ - 1:c_, :] for hh in hs]
        k_dec = [(kf[hh] * jnp.exp(g_last[hh] - gc[hh])).astype(BF16) for hh in hs]
        for hh in hs:
            state_ref[hh] = state[hh] * jnp.exp(g_last[hh]) + lax.dot_general(
                k_dec[hh], v_new[hh], contract_first, preferred_element_type=F32)
        for hh in hs:
            zf = z_ref[rows, cols[hh]]
            on = o[hh] * lax.rsqrt(jnp.mean(o[hh] * o[hh], -1, keepdims=True) + RMS_EPS) * nw
            o_ref[rows, cols[hh]] = (on * (zf * jax.nn.sigmoid(zf))).astype(BF16)
        return carry

    lax.fori_loop(0, n_chunks, chunk_body, 0)


def deltanet(qkv, p1, gates_col, gates_row, a_log, dt_bias, norm_w, layer, dn_width, bs=512):
    s = qkv.shape[0]
    hb = DN_HEADS_PER_STEP
    c_ = DN_CHUNK
    heads = dn_width // DN_HEAD_DIM
    assert heads % hb == 0
    groups = heads // hb
    bs = min(bs, s)
    assert s % bs == 0 and bs % c_ == 0
    wblk = hb * DN_HEAD_DIM
    ncb = dn_width // wblk
    n_chunks = bs // c_
    a_log_c = a_log.reshape(-1, groups, 1, hb)
    a_log_r = a_log.reshape(-1, groups, hb, 1)
    dt_c = dt_bias.reshape(-1, groups, 1, hb)
    dt_r = dt_bias.reshape(-1, groups, hb, 1)
    nw = norm_w.reshape(-1, 1, DN_HEAD_DIM)

    def seg(which):
        return pl.BlockSpec((bs, wblk), lambda g, i: (i, which * ncb + g))

    col_spec = lambda w: pl.BlockSpec((None, None, bs, hb), lambda g, i: (w, g, i, 0))
    row_spec = lambda w: pl.BlockSpec((None, None, n_chunks, hb, c_), lambda g, i: (w, g, i, 0, 0))
    pc_spec = pl.BlockSpec((None, None, 1, hb), lambda g, i: (layer, g, 0, 0))
    pr_spec = pl.BlockSpec((None, None, hb, 1), lambda g, i: (layer, g, 0, 0))
    vmem = 2 * bs * wblk * (2 * 3 + 4 + 2) + hb * DN_HEAD_DIM * DN_HEAD_DIM * 4 + 16 * bs * V7X_LANES * 4
    return pl.pallas_call(
        functools.partial(_dn_kernel, hb=hb, chunk=c_, n_chunks=n_chunks),
        out_shape=jax.ShapeDtypeStruct((s, dn_width), BF16),
        grid=(groups, s // bs),
        in_specs=[seg(0), seg(1), seg(2), seg(3),
                  col_spec(0), col_spec(1), row_spec(0), row_spec(1),
                  pc_spec, pc_spec, pr_spec, pr_spec,
                  pl.BlockSpec((None, 1, DN_HEAD_DIM), lambda g, i: (layer, 0, 0))],
        out_specs=pl.BlockSpec((bs, wblk), lambda g, i: (i, g)),
        scratch_shapes=[pltpu.VMEM((hb, DN_HEAD_DIM, DN_HEAD_DIM), F32)],
        compiler_params=_params(("parallel", "arbitrary"), vmem),
        name="deltanet",
    )(qkv, qkv, qkv, p1, gates_col, gates_col, gates_row, gates_row, a_log_c, dt_c, a_log_r, dt_r, nw)


def _da_kernel(lam_init_ref, q_ref, k_ref, v_ref, lq1_ref, lk1_ref, lq2_ref, lk2_ref, nw_ref, o_ref,
               acc_ref, m_ref, l_ref, s_a, s_b, mx_a, mx_b, *, blk):
    iq = pl.program_id(1)
    d_ = DA_HEAD_DIM
    rc = DA_ROW_CHUNK
    contract_last = (((1,), (1,)), ((), ()))
    m_ref[...] = jnp.full_like(m_ref, -jnp.inf)
    l_ref[...] = jnp.zeros_like(l_ref)
    acc_ref[...] = jnp.zeros_like(acc_ref)
    chains = [(c, sub) for c in range(blk // rc) for sub in range(2)]
    rsl = lambda c: slice(c * rc, (c + 1) * rc)
    lanes = lambda x, width: jnp.concatenate([x] * (width // V7X_LANES), axis=1)

    def scores(j, out):
        s_out, mx_out = out
        kb = k_ref[pl.ds(pl.multiple_of(j * blk, blk), blk), :]
        for sub in range(2):
            s_big = lax.dot_general(q_ref[:, sub * d_:(sub + 1) * d_], kb[:, sub * d_:(sub + 1) * d_],
                                    contract_last, preferred_element_type=F32)
            s_out[sub] = s_big
            mx_out[sub] = jnp.broadcast_to(jnp.max(s_big, -1, keepdims=True), (blk, V7X_LANES))

    def kv_step(j, cur, nxt, masked):
        if nxt is not None:
            scores(j + 1, nxt)
        s_cur, mx_cur = cur
        vb = v_ref[pl.ds(pl.multiple_of(j * blk, blk), blk), :]
        s = [s_cur[sub, rsl(c), :] for c, sub in chains]
        if masked:
            row = lax.broadcasted_iota(I32, (rc, blk), 0)
            col = lax.broadcasted_iota(I32, (rc, blk), 1)
            s = [jnp.where(col <= row + c * rc, s[n], -jnp.inf) for n, (c, sub) in enumerate(chains)]
            m_cur = [jnp.max(s[n], -1, keepdims=True) for n in range(len(chains))]
        else:
            m_cur = [mx_cur[sub, rsl(c), :] for c, sub in chains]
        alpha, pb = [], []
        for n, (c, sub) in enumerate(chains):
            m_old = m_ref[sub, rsl(c), :]
            m_new = jnp.maximum(m_old, m_cur[n])
            a_n = jnp.exp2(m_old - m_new)
            p = jnp.exp2(s[n] - lanes(m_new, blk))
            m_ref[sub, rsl(c), :] = m_new
            l_ref[sub, rsl(c), :] = a_n * l_ref[sub, rsl(c), :] + jnp.sum(p, -1, keepdims=True)
            alpha.append(a_n)
            pb.append(p.astype(BF16))
        for sub in range(2):
            mine = [n for n, (c, sb) in enumerate(chains) if sb == sub]
            p_sub = jnp.concatenate([pb[n] for n in mine], axis=0)
            a_sub = jnp.concatenate([alpha[n] for n in mine], axis=0)
            acc_ref[sub] = lanes(a_sub, DA_V_DIM) * acc_ref[sub] + _dot32(p_sub, vb)

    odd = lax.rem(iq, 2)
    buf_a, buf_b = (s_a, mx_a), (s_b, mx_b)

    @pl.when(odd == 0)
    def _even_start():
        scores(0, buf_a)

    @pl.when(odd == 1)
    def _odd_start():
        scores(0, buf_b)
        kv_step(0, buf_b, buf_a, False)

    def pair(jj, carry):
        j = odd + 2 * jj
        kv_step(j, buf_a, buf_b, False)
        kv_step(j + 1, buf_b, buf_a, False)
        return carry

    lax.fori_loop(0, (iq - odd) // 2, pair, 0)
    kv_step(iq, buf_a, None, True)

    lam_init = lam_init_ref[0]
    lam = (jnp.exp(jnp.sum(lq1_ref[...] * lk1_ref[...], keepdims=True))
           - jnp.exp(jnp.sum(lq2_ref[...] * lk2_ref[...], keepdims=True)) + lam_init)
    o = acc_ref[0] / lanes(l_ref[0], DA_V_DIM) - lam * (acc_ref[1] / lanes(l_ref[1], DA_V_DIM))
    on = o * lax.rsqrt(jnp.mean(o * o, -1, keepdims=True) + RMS_EPS) * nw_ref[...]
    o_ref[...] = (on * (1.0 - lam_init)).astype(BF16)


def diff_attention(p2, width, lq1, lk1, lq2, lk2, norm_w, lam_init, layer, blk=512):
    s = p2.shape[0]
    heads = width // DA_V_DIM
    blk = min(blk, s)
    assert s % blk == 0
    vec = pl.BlockSpec((None, 1, DA_HEAD_DIM), lambda h, i, *_: (layer, 0, 0))
    full = lambda seg: pl.BlockSpec((s, DA_V_DIM), lambda h, i, *_: (0, seg * heads + h))
    tile = pl.BlockSpec((blk, DA_V_DIM), lambda h, i, *_: (i, h))
    grid_spec = pltpu.PrefetchScalarGridSpec(
        num_scalar_prefetch=1,
        grid=(heads, s // blk),
        in_specs=[tile, full(1), full(2), vec, vec, vec, vec,
                  pl.BlockSpec((None, 1, DA_V_DIM), lambda h, i, *_: (layer, 0, 0))],
        out_specs=tile,
        scratch_shapes=[pltpu.VMEM((2, blk, DA_V_DIM), F32), pltpu.VMEM((2, blk, V7X_LANES), F32),
                        pltpu.VMEM((2, blk, V7X_LANES), F32),
                        pltpu.VMEM((2, blk, blk), F32), pltpu.VMEM((2, blk, blk), F32),
                        pltpu.VMEM((2, blk, V7X_LANES), F32), pltpu.VMEM((2, blk, V7X_LANES), F32)],
    )
    vmem = 4 * s * DA_V_DIM * 2 + 4 * blk * DA_V_DIM * 2 + 2 * blk * DA_V_DIM * 4 + 8 * blk * blk * 4
    return pl.pallas_call(
        functools.partial(_da_kernel, blk=blk),
        out_shape=jax.ShapeDtypeStruct((s, width), BF16),
        grid_spec=grid_spec,
        compiler_params=_params(("parallel", "arbitrary"), vmem),
        name="diff_attention",
    )(lam_init, p2, p2, p2, lq1, lk1, lq2, lk2, norm_w)


def _mem_attn_kernel(q_ref, k_ref, v_ref, o_ref, *, scale):
    q = (q_ref[...].astype(F32) * scale).astype(BF16)
    k = k_ref[...].astype(BF16)
    sc = lax.dot_general(q, k, (((1,), (1,)), ((), ())), preferred_element_type=F32)
    sc = sc - jnp.max(sc, -1, keepdims=True)
    p = jnp.exp(sc)
    denom = jnp.sum(p, -1, keepdims=True)
    o = jnp.dot(p.astype(BF16), v_ref[...].astype(BF16), preferred_element_type=F32) / denom
    o_ref[...] = o.astype(BF16)


def memory_attention(p2, mem_kv, q_col0, mem_width, bt=512):
    s = p2.shape[0]
    n_mem = mem_kv.shape[0]
    hd = mem_width // MEM_HEADS
    bt = min(bt, s)
    assert q_col0 % hd == 0
    qb0 = q_col0 // hd
    return pl.pallas_call(
        functools.partial(_mem_attn_kernel, scale=hd ** -0.5),
        out_shape=jax.ShapeDtypeStruct((s, mem_width), BF16),
        grid=(s // bt, MEM_HEADS),
        in_specs=[pl.BlockSpec((bt, hd), lambda i, h: (i, qb0 + h)),
                  pl.BlockSpec((n_mem, hd), lambda i, h: (0, h)),
                  pl.BlockSpec((n_mem, hd), lambda i, h: (0, MEM_HEADS + h))],
        out_specs=pl.BlockSpec((bt, hd), lambda i, h: (i, h)),
        compiler_params=_params(("parallel", "parallel"), 4 * bt * hd * 4 + 4 * n_mem * hd * 4 + 4 * bt * n_mem * 4),
        name="memory_attention",
    )(p2, mem_kv, mem_kv)


def _router_kernel(x_ref, w_ref, b_ref, idx_ref, gate_ref, rank_ref, cnt_ref, carry_ref):
    i = pl.program_id(0)

    @pl.when(i == 0)
    def _reset():
        carry_ref[...] = jnp.zeros_like(carry_ref)

    x = x_ref[...]
    w = w_ref[...]
    x_hi = x.astype(BF16)
    w_hi = w.astype(BF16)
    x_lo = (x - x_hi.astype(F32)).astype(BF16)
    w_lo = (w - w_hi.astype(F32)).astype(BF16)
    logits = _dot32(x_hi, w_hi) + (_dot32(x_lo, w_hi) + _dot32(x_hi, w_lo)) + b_ref[...]
    bt, n_e = logits.shape
    lane = lax.broadcasted_iota(I32, (bt, n_e), 1)
    work = logits
    vals, sels = [], []
    for kk in range(TOP_K):
        mx = jnp.max(work, -1, keepdims=True)
        idx = jnp.min(jnp.where(work == mx, lane, n_e), -1, keepdims=True)
        sel = lane == idx
        vals.append(mx)
        sels.append(sel)
        idx_ref[:, kk:kk + 1] = idx
        work = jnp.where(sel, -jnp.inf, work)
    exps = [jnp.exp(v - vals[0]) for v in vals]
    denom = exps[0]
    for e in exps[1:]:
        denom = denom + e
    for kk in range(TOP_K):
        gate_ref[:, kk:kk + 1] = exps[kk] / denom
    mask = sels[0]
    for sel in sels[1:]:
        mask = mask | sel
    maskf = mask.astype(F32)
    row = lax.broadcasted_iota(I32, (bt, bt), 0)
    col = lax.broadcasted_iota(I32, (bt, bt), 1)
    before = (row > col).astype(BF16)
    rank = carry_ref[...] + jnp.dot(before, maskf.astype(BF16), preferred_element_type=F32)
    for kk in range(TOP_K):
        rank_ref[:, kk:kk + 1] = jnp.sum(jnp.where(sels[kk], rank, 0.0), -1, keepdims=True).astype(I32)
    carry_ref[...] = carry_ref[...] + jnp.sum(maskf, 0, keepdims=True)
    cnt_ref[...] = carry_ref[...].astype(I32)


def router(x, router_w, router_b, layer, bt=256):
    s, d = x.shape
    n_e = router_w.shape[-1]
    bt = min(bt, s)
    small = lambda dt: jax.ShapeDtypeStruct((s, TOP_K), dt)
    tok = pl.BlockSpec((bt, TOP_K), lambda i: (i, 0))
    return pl.pallas_call(
        _router_kernel,
        out_shape=(small(I32), small(F32), small(I32), jax.ShapeDtypeStruct((1, n_e), I32)),
        grid=(s // bt,),
        in_specs=[pl.BlockSpec((bt, d), lambda i: (i, 0)),
                  pl.BlockSpec((None, d, n_e), lambda i: (layer, 0, 0)),
                  pl.BlockSpec((None, 1, n_e), lambda i: (layer, 0, 0))],
        out_specs=(tok, tok, tok, pl.BlockSpec((1, n_e), lambda i: (0, 0))),
        scratch_shapes=[pltpu.VMEM((1, n_e), F32)],
        compiler_params=_params(("arbitrary",), 2 * bt * d * 4 + 2 * d * V7X_LANES * 4),
        name="router",
    )(x, router_w, router_b)


def _row_copy(src, src_row, dst, dst_row, sem):
    return pltpu.make_async_copy(src.at[pl.ds(src_row, 1)], dst.at[pl.ds(dst_row, 1)], sem)


def _moe_kernel(te_ref, nt_ref, src_next_ref, dst_prev_ref, dst_ref, src_ref, x_hbm, wg_ref, wu_ref, wd_ref,
                bg_ref, bu_ref, bd_ref, y_hbm, xbuf, ybuf, wgb, wub, wdb, gsem, ssem, *, tm):
    i = pl.program_id(0)
    nt = nt_ref[0]
    slot = lax.rem(i, 2)
    other = 1 - slot
    yslot = lax.rem(i, 3)
    yprev = lax.rem(i + 2, 3)
    ynext = lax.rem(i + 1, 3)
    groups = MOE_DMA_GROUPS
    bounds = [round(g * tm / groups) for g in range(groups + 1)]

    def start_gather(idx_ref, s_, g):
        for r in range(bounds[g], bounds[g + 1]):
            _row_copy(x_hbm, idx_ref[0, r], xbuf.at[s_], r, gsem.at[s_]).start(priority=r % 2)

    def start_scatter(idx_ref, s_, g):
        for r in range(bounds[g], bounds[g + 1]):
            _row_copy(ybuf.at[s_], r, y_hbm, idx_ref[0, r], ssem.at[s_]).start(priority=r % 2)

    def wait_gather(s_):
        pltpu.make_async_copy(x_hbm.at[pl.ds(0, tm)], xbuf.at[s_], gsem.at[s_]).wait()

    def wait_scatter(s_):
        pltpu.make_async_copy(ybuf.at[s_], y_hbm.at[pl.ds(0, tm)], ssem.at[s_]).wait()

    @pl.when(i == 0)
    def _prologue():
        ybuf[yprev] = jnp.zeros((tm, ybuf.shape[2]), U32)
        for region in range(2):
            first = y_hbm.shape[0] - MOE_SPARE_REGIONS * tm + region * tm
            spare = pltpu.make_async_copy(ybuf.at[yprev], y_hbm.at[pl.ds(first, tm)], ssem.at[yprev])
            spare.start()
            spare.wait()
        for g in range(groups):
            start_gather(src_ref, slot, g)

    @pl.when(i < nt)
    def _tile():
        wait_gather(slot)
        prev = te_ref[jnp.maximum(i - 1, 0)]

        @pl.when((i == 0) | (te_ref[i] != prev))
        def _cast_weights():
            wgb[...] = wg_ref[...].astype(BF16)
            wub[...] = wu_ref[...].astype(BF16)
            wdb[...] = wd_ref[...].astype(BF16)

        half = xbuf.shape[2]
        x_lo, x_hi = _unpack_bf16_pair(xbuf[slot])
        x_lo, x_hi = x_lo.astype(BF16), x_hi.astype(BF16)
        gate = _dot32(x_lo, wgb[:half, :])
        start_gather(src_next_ref, other, 0)
        up = _dot32(x_lo, wub[:half, :])
        start_gather(src_next_ref, other, 1)
        gate = gate + _dot32(x_hi, wgb[half:, :]) + bg_ref[...]
        start_gather(src_next_ref, other, 2)
        up = up + _dot32(x_hi, wub[half:, :]) + bu_ref[...]
        start_scatter(dst_prev_ref, yprev, 0)
        gate = jnp.minimum(gate, SWIGLU_LIMIT)
        up = jnp.clip(up, -SWIGLU_LIMIT, SWIGLU_LIMIT)
        act = ((up + 1.0) * gate * jax.nn.sigmoid(SWIGLU_ALPHA * gate)).astype(BF16)
        y_lo = _dot32(act, wdb[:, :half]) + bd_ref[:, :half]
        start_scatter(dst_prev_ref, yprev, 1)
        y_hi = _dot32(act, wdb[:, half:]) + bd_ref[:, half:]
        start_scatter(dst_prev_ref, yprev, 2)
        ybuf[yslot] = _pack_bf16_pair(y_lo, y_hi)

        @pl.when(i > 0)
        def _drain_older():
            wait_scatter(ynext)

        @pl.when(i == nt - 1)
        def _drain_last():
            wait_gather(other)
            wait_scatter(yprev)
            for g in range(groups):
                start_scatter(dst_ref, yslot, g)
            wait_scatter(yslot)


def routed_experts(x, tile_expert, n_tiles, src_tok, dst_row, w_gate, b_gate, w_up, b_up, w_down, b_down, layer):
    s, half = x.shape
    d = 2 * half
    f = w_gate.shape[-1]
    tm = MOE_TILE
    nt = tile_expert.shape[0]
    idx_spec = pl.BlockSpec((None, 1, tm), lambda i, *_: (i, 0, 0), memory_space=pltpu.SMEM)
    idx_next = pl.BlockSpec((None, 1, tm), lambda i, *_: (jnp.minimum(i + 1, nt - 1), 0, 0),
                            memory_space=pltpu.SMEM)
    idx_prev = pl.BlockSpec((None, 1, tm), lambda i, *_: (i, 0, 0), memory_space=pltpu.SMEM)
    idx_own = pl.BlockSpec((None, 1, tm), lambda i, *_: (i + 1, 0, 0), memory_space=pltpu.SMEM)
    wspec = lambda a, b: pl.BlockSpec((None, None, a, b), lambda i, te, n: (layer, te[i], 0, 0))
    grid_spec = pltpu.PrefetchScalarGridSpec(
        num_scalar_prefetch=2,
        grid=(nt,),
        in_specs=[idx_next, idx_prev, idx_own, idx_spec, pl.BlockSpec(memory_space=pl.ANY),
                  wspec(d, f), wspec(d, f), wspec(f, d), wspec(1, f), wspec(1, f), wspec(1, d)],
        out_specs=pl.BlockSpec(memory_space=pl.ANY),
        scratch_shapes=[pltpu.VMEM((2, tm, half), U32), pltpu.VMEM((3, tm, half), U32),
                        pltpu.VMEM((d, f), BF16), pltpu.VMEM((d, f), BF16), pltpu.VMEM((f, d), BF16),
                        pltpu.SemaphoreType.DMA((2,)), pltpu.SemaphoreType.DMA((3,))],
    )
    vmem = 2 * 3 * d * f * 4 + 3 * d * f * 2 + 5 * tm * half * 4 + 4 * tm * d * 4
    return pl.pallas_call(
        functools.partial(_moe_kernel, tm=tm),
        out_shape=jax.ShapeDtypeStruct((TOP_K * s + MOE_SPARE_REGIONS * tm, half), U32),
        grid_spec=grid_spec,
        compiler_params=_params(("arbitrary",), vmem),
        name="routed_experts",
    )(tile_expert, n_tiles, src_tok, dst_row, dst_row, src_tok, x, w_gate, w_up, w_down, b_gate, b_up, b_down)


def _combine_ln_kernel(x_ref, y0_ref, y1_ref, y2_ref, y3_ref, gate_ref, g_ref, b_ref, o_ref, ob_ref, *, alpha):
    half = y0_ref.shape[1]
    gates = gate_ref[...]
    ffn_lo = ffn_hi = None
    for kk, y_ref in enumerate((y0_ref, y1_ref, y2_ref, y3_ref)):
        y_lo, y_hi = _unpack_bf16_pair(y_ref[...])
        gk = gates[:, kk:kk + 1]
        ffn_lo = gk * y_lo if ffn_lo is None else ffn_lo + gk * y_lo
        ffn_hi = gk * y_hi if ffn_hi is None else ffn_hi + gk * y_hi
    o_lo, o_hi = _layer_norm_halves(alpha * x_ref[:, :half] + ffn_lo, alpha * x_ref[:, half:] + ffn_hi,
                                    g_ref[...], b_ref[...])
    o_ref[:, :half] = o_lo
    o_ref[:, half:] = o_hi
    ob_ref[:, :half] = o_lo.astype(BF16)
    ob_ref[:, half:] = o_hi.astype(BF16)


def combine_layer_norm(x, y, gates, g, b, layer, alpha, bt=256):
    s, d = x.shape
    half = d // 2
    bt = min(bt, s)
    nb = s // bt
    row = pl.BlockSpec((bt, d), lambda i: (i, 0))
    yk = lambda kk: pl.BlockSpec((bt, half), lambda i: (kk * nb + i, 0))
    par = pl.BlockSpec((None, 1, d), lambda i: (layer, 0, 0))
    return pl.pallas_call(
        functools.partial(_combine_ln_kernel, alpha=alpha),
        out_shape=(jax.ShapeDtypeStruct((s, d), F32), jax.ShapeDtypeStruct((s, d), BF16)),
        grid=(nb,),
        in_specs=[row, yk(0), yk(1), yk(2), yk(3), pl.BlockSpec((bt, TOP_K), lambda i: (i, 0)), par, par],
        out_specs=(row, row),
        compiler_params=_params(("parallel",), 2 * bt * d * (4 + 4 * 2 + 4 + 2)),
        name="combine_ln",
    )(x, y, y, y, y, gates, g, b)


def _routing_plan(idx, rank, counts, s, n_e):
    tm = MOE_TILE
    nt = s * TOP_K // tm + n_e
    counts = counts.reshape(n_e)
    tiles_e = (counts + tm - 1) // tm
    tile_end = jnp.cumsum(tiles_e)
    tile_start = tile_end - tiles_e
    n_tiles = tile_end[-1]
    tile_ids = jnp.arange(nt, dtype=I32)
    te = jnp.sum((tile_end[None, :] <= jnp.minimum(tile_ids, n_tiles - 1)[:, None]).astype(I32), axis=1)
    te = jnp.clip(te, 0, n_e - 1).astype(I32)
    pos = (tile_start * tm)[idx] + rank
    tok = jnp.arange(s, dtype=I32)[:, None]
    val = jnp.arange(TOP_K, dtype=I32)[None, :] * s + tok
    row_in_tile = jnp.arange(tm, dtype=I32)
    spare = TOP_K * s + (tile_ids % 2)[:, None] * tm + row_in_tile[None, :]
    slots = spare.reshape(-1).at[pos.reshape(-1)].set(val.reshape(-1), unique_indices=True)
    src_tok = (slots % s).reshape(nt, 1, tm)
    zero_tile_rows = TOP_K * s + 2 * tm + row_in_tile
    dst_row = jnp.concatenate([zero_tile_rows, slots]).reshape(nt + 1, 1, tm)
    return te, n_tiles.reshape(1).astype(I32), src_tok, dst_row


def kernel(x, mem, positions, w_in, conv_w, a_log, dt_bias, dn_norm_w, lambda_q1, lambda_k1, lambda_q2,
           lambda_k2, da_norm_w, w_mem_kv, w_o, ln1_g, ln1_b, router_w, router_b, w_gate, b_gate, w_up, b_up,
           w_down, b_down, ln2_g, ln2_b):
    bsz, s, d = x.shape
    assert bsz == 1
    depth = w_in.shape[0]
    n_e = router_w.shape[-1]
    f = w_gate.shape[-1]
    dn_w = 3 * d // 8
    da_w = 3 * d // 8
    mem_w = d - dn_w - da_w
    dn_heads = dn_w // DN_HEAD_DIM
    alpha = (2 * depth) ** 0.25
    hb = DN_HEADS_PER_STEP
    groups = dn_heads // hb
    c_ = DN_CHUNK

    inv = 1.0 / (ROPE_THETA ** (jnp.arange(0, DA_HEAD_DIM, 2, dtype=F32) / DA_HEAD_DIM))
    ang = positions[0].astype(F32)[:, None] * inv
    ang = jnp.concatenate([ang, ang], -1)
    sign = jnp.concatenate([-jnp.ones((DA_HEAD_DIM // 2,), F32), jnp.ones((DA_HEAD_DIM // 2,), F32)])
    cos, sin_signed = jnp.cos(ang), jnp.sin(ang) * sign
    gate0 = 4 * dn_w
    post0 = gate0 + 2 * dn_heads
    w_in_t = jnp.swapaxes(w_in, 1, 2)
    lam_inits = [0.8 - 0.6 * math.exp(-0.3 * layer) for layer in range(depth)]
    vec3 = lambda p: p.reshape(depth, 1, -1)
    ln1_g3, ln1_b3, ln2_g3, ln2_b3 = vec3(ln1_g), vec3(ln1_b), vec3(ln2_g), vec3(ln2_b)
    lq1, lk1, lq2, lk2 = vec3(lambda_q1), vec3(lambda_k1), vec3(lambda_q2), vec3(lambda_k2)
    da_nw = vec3(da_norm_w)
    router_b3 = vec3(router_b)
    b_gate4 = b_gate.reshape(depth, n_e, 1, f)
    b_up4 = b_up.reshape(depth, n_e, 1, f)
    b_down4 = b_down.reshape(depth, n_e, 1, d)

    xf = x[0]
    xb = cast_bf16(xf)
    memb = cast_bf16(mem[0])
    for layer in range(depth):
        p1 = matmul_nt(xb, w_in_t, layer, 4 * dn_w, name="proj_dn")
        rope = (cos, sin_signed, 2 * da_w, da_w, DA_HEAD_DIM ** -0.5 * LOG2E)
        p2 = matmul_nt(xb, w_in_t, layer, 3 * da_w + mem_w, row0=post0, out_dtype=BF16, rope=rope,
                       name="proj_attn")
        gts = matmul_nt(xb, w_in_t, layer, V7X_LANES, row0=gate0, bn=V7X_LANES, name="proj_gates")
        mem_kv = matmul([memb], w_mem_kv, layer, 2 * mem_w, name="proj_mem")

        ab = gts[:, :2 * dn_heads].reshape(s, 2, groups, hb)
        gates_col = ab.transpose(1, 2, 0, 3)
        gates_row = ab.reshape(s // c_, c_, 2, groups, hb).transpose(2, 3, 0, 4, 1)
        qkv = dn_prep(p1, conv_w, layer, dn_w)
        y_dn = deltanet(qkv, p1, gates_col, gates_row, a_log, dt_bias, dn_norm_w, layer, dn_w)

        lam_init = jnp.full((1,), lam_inits[layer], F32)
        y_da = diff_attention(p2, da_w, lq1, lk1, lq2, lk2, da_nw, lam_init, layer)
        y_mem = memory_attention(p2, mem_kv, 3 * da_w, mem_w)

        mixed = matmul([y_dn, y_da, y_mem], w_o, layer, d, name="proj_out")
        x1, x1_packed = residual_layer_norm(xf, mixed, ln1_g3, ln1_b3, layer, alpha)

        idx, gates, rank, counts = router(x1, router_w, router_b3, layer)
        te, n_tiles, src_tok, dst_row = _routing_plan(idx, rank, counts, s, n_e)
        y = routed_experts(x1_packed, te, n_tiles, src_tok, dst_row, w_gate, b_gate4, w_up, b_up4, w_down, b_down4,
                           layer)
        xf, xb = combine_layer_norm(x1, y, gates, ln2_g3, ln2_b3, layer, alpha)
    return xf[None]
```

```python
import functools
import math

import jax
import jax.numpy as jnp
from jax import lax
from jax.experimental import pallas as pl
from jax.experimental.pallas import tpu as pltpu

F32 = jnp.float32
BF16 = jnp.bfloat16
I32 = jnp.int32
U32 = jnp.uint32
HIGHEST = lax.Precision.HIGHEST
HIGH_HALF = 0xFFFF0000

V7X_LANES = 128
V7X_SUBLANES = 8
V7X_VMEM_BYTES = 64 * 1024 * 1024
VMEM_COMPILER_MARGIN = 6 * 1024 * 1024

DN_HEAD_DIM = 128
CONV_K = 4
DA_HEAD_DIM = 128
DA_V_DIM = 2 * DA_HEAD_DIM
MEM_HEADS = 4
TOP_K = 4
ROPE_THETA = 10000.0
SWIGLU_LIMIT = 7.0
SWIGLU_ALPHA = 1.702
LN_EPS = 1e-5
RMS_EPS = 1e-6
L2_EPS = 1e-6
LOG2E = 1.4426950408889634

DN_CHUNK = 128
DN_HEADS_PER_STEP = 12
DN_INV_BASE = 16
MOE_TILE = 256
MOE_DMA_GROUPS = 3
MOE_SPARE_REGIONS = 3
MM_CAST_COLS = 512
DA_ROW_CHUNK = 128


def _vmem_limit(nbytes):
    return int(min(nbytes + VMEM_COMPILER_MARGIN, V7X_VMEM_BYTES - 2 * 1024 * 1024))


def _params(semantics, vmem_bytes):
    return pltpu.CompilerParams(dimension_semantics=semantics, vmem_limit_bytes=_vmem_limit(vmem_bytes))


def _cast_kernel(x_ref, o_ref):
    o_ref[...] = x_ref[...].astype(o_ref.dtype)


def cast_bf16(x, bt=256):
    m, d = x.shape
    bt = min(bt, m)
    return pl.pallas_call(
        _cast_kernel,
        out_shape=jax.ShapeDtypeStruct((m, d), BF16),
        grid=(m // bt,),
        in_specs=[pl.BlockSpec((bt, d), lambda i: (i, 0))],
        out_specs=pl.BlockSpec((bt, d), lambda i: (i, 0)),
        compiler_params=_params(("parallel",), 2 * bt * d * 6),
        name="cast_bf16",
    )(x)


def _mm_kernel(*refs, k_splits):
    n_a = len(k_splits)
    a_refs, w_ref, o_ref, wb_ref = refs[:n_a], refs[n_a], refs[n_a + 1], refs[n_a + 2]

    @pl.when(pl.program_id(1) == 0)
    def _cast_weights():
        wb_ref[...] = w_ref[...].astype(BF16)

    acc = None
    k0 = 0
    for a_ref, kk in zip(a_refs, k_splits):
        part = jnp.dot(a_ref[...], wb_ref[k0:k0 + kk, :], preferred_element_type=F32)
        acc = part if acc is None else acc + part
        k0 += kk
    o_ref[...] = acc.astype(o_ref.dtype)


def matmul(a_list, w, layer, ncols, *, bm=1024, bn=512, out_dtype=F32, name="proj"):
    m = a_list[0].shape[0]
    k_splits = tuple(a.shape[1] for a in a_list)
    k = sum(k_splits)
    assert w.shape[1] == k and ncols % bn == 0
    bm = min(bm, m)
    assert m % bm == 0
    in_specs = [pl.BlockSpec((bm, kk), lambda j, i: (i, 0)) for kk in k_splits]
    in_specs.append(pl.BlockSpec((None, k, bn), lambda j, i: (layer, 0, j)))
    vmem = 2 * bm * k * 2 + 2 * k * bn * 4 + k * bn * 2 + 2 * bm * bn * 4
    return pl.pallas_call(
        functools.partial(_mm_kernel, k_splits=k_splits),
        out_shape=jax.ShapeDtypeStruct((m, ncols), out_dtype),
        grid=(ncols // bn, m // bm),
        in_specs=in_specs,
        out_specs=pl.BlockSpec((bm, bn), lambda j, i: (i, j)),
        scratch_shapes=[pltpu.VMEM((k, bn), BF16)],
        compiler_params=_params(("arbitrary", "arbitrary"), vmem),
        name=name,
    )(*a_list, w)


def _mm_nt_kernel(*refs, row_shift, rope_tiles, rope_scaled_tiles, rope_scale):
    refs = list(refs)
    a_ref, w_ref = refs[:2]
    w_tail_ref = refs[2] if row_shift else None
    cos_ref, sin_ref = (refs[-4], refs[-3]) if rope_tiles else (None, None)
    o_ref, wb_ref = refs[-2:]

    @pl.when(pl.program_id(1) == 0)
    def _cast_weights():
        bn, k = wb_ref.shape
        if not row_shift:
            wb_ref[...] = w_ref[...].astype(BF16)
        else:
            for c in range(k // MM_CAST_COLS):
                cols = slice(c * MM_CAST_COLS, (c + 1) * MM_CAST_COLS)
                both = jnp.concatenate([w_ref[:, cols], w_tail_ref[:, cols]], axis=0)
                wb_ref[:, cols] = both[row_shift:row_shift + bn].astype(BF16)

    acc = lax.dot_general(a_ref[...], wb_ref[...], (((1,), (1,)), ((), ())), preferred_element_type=F32)
    if not rope_tiles:
        o_ref[...] = acc.astype(o_ref.dtype)
    else:
        j = pl.program_id(0)

        @pl.when(j < rope_tiles)
        def _rotary():
            c = cos_ref[...]
            s = sin_ref[...]
            scale = jnp.where(j < rope_scaled_tiles, rope_scale, 1.0).astype(F32)
            for t in range(acc.shape[1] // DA_HEAD_DIM):
                sl = slice(t * DA_HEAD_DIM, (t + 1) * DA_HEAD_DIM)
                xh = acc[:, sl]
                o_ref[:, sl] = ((xh * c + pltpu.roll(xh, DA_HEAD_DIM // 2, 1) * s) * scale).astype(o_ref.dtype)

        @pl.when(j >= rope_tiles)
        def _plain():
            o_ref[...] = acc.astype(o_ref.dtype)


def matmul_nt(a, w_t, layer, ncols, *, row0=0, bm=1024, bn=512, out_dtype=F32, rope=None, name="proj_t"):
    m, k = a.shape
    assert w_t.shape[2] == k and ncols % bn == 0 and k % MM_CAST_COLS == 0
    bm = min(bm, m)
    assert m % bm == 0
    row_block0, row_shift = divmod(row0, bn)
    assert row_shift < V7X_LANES and row_shift % V7X_SUBLANES == 0
    in_specs = [pl.BlockSpec((bm, k), lambda j, i: (i, 0)),
                pl.BlockSpec((None, bn, k), lambda j, i: (layer, j + row_block0, 0))]
    operands = [a, w_t]
    if row_shift:
        per = bn // V7X_LANES
        in_specs.append(pl.BlockSpec((None, V7X_LANES, k), lambda j, i: (layer, (j + row_block0 + 1) * per, 0)))
        operands.append(w_t)
    rope_tiles = rope_scaled_tiles = 0
    rope_scale = 1.0
    if rope is not None:
        cos, sin_signed, n_rope_cols, n_scaled_cols, rope_scale = rope
        assert n_rope_cols % bn == 0 and n_scaled_cols % bn == 0
        rope_tiles, rope_scaled_tiles = n_rope_cols // bn, n_scaled_cols // bn
        tab = pl.BlockSpec((bm, DA_HEAD_DIM), lambda j, i: (i, 0))
        in_specs += [tab, tab]
        operands += [cos, sin_signed]
    vmem = 2 * bm * k * 2 + 2 * k * (bn + V7X_LANES) * 4 + k * bn * 2 + 2 * bm * bn * 4 + 4 * bm * DA_HEAD_DIM * 4
    return pl.pallas_call(
        functools.partial(_mm_nt_kernel, row_shift=row_shift, rope_tiles=rope_tiles,
                          rope_scaled_tiles=rope_scaled_tiles, rope_scale=rope_scale),
        out_shape=jax.ShapeDtypeStruct((m, ncols), out_dtype),
        grid=(ncols // bn, m // bm),
        in_specs=in_specs,
        out_specs=pl.BlockSpec((bm, bn), lambda j, i: (i, j)),
        scratch_shapes=[pltpu.VMEM((bn, k), BF16)],
        compiler_params=_params(("arbitrary", "arbitrary"), vmem),
        name=name,
    )(*operands)


def _pack_bf16_pair(lo, hi):
    lo_bits = lax.bitcast_convert_type(lo.astype(BF16).astype(F32), U32)
    hi_bits = lax.bitcast_convert_type(hi.astype(BF16).astype(F32), U32)
    return (lo_bits >> 16) | (hi_bits & U32(HIGH_HALF))


def _unpack_bf16_pair(w):
    return lax.bitcast_convert_type(w << 16, F32), lax.bitcast_convert_type(w & U32(HIGH_HALF), F32)


def _layer_norm_halves(h_lo, h_hi, g, b):
    half = h_lo.shape[1]
    n = 2 * half
    mu = (jnp.sum(h_lo, -1, keepdims=True) + jnp.sum(h_hi, -1, keepdims=True)) / n
    d_lo, d_hi = h_lo - mu, h_hi - mu
    var = (jnp.sum(d_lo * d_lo, -1, keepdims=True) + jnp.sum(d_hi * d_hi, -1, keepdims=True)) / n
    inv = lax.rsqrt(var + LN_EPS)
    return d_lo * inv * g[:, :half] + b[:, :half], d_hi * inv * g[:, half:] + b[:, half:]


def _ln_kernel(x_ref, y_ref, g_ref, b_ref, o_ref, op_ref, *, alpha):
    half = op_ref.shape[1]
    o_lo, o_hi = _layer_norm_halves(alpha * x_ref[:, :half] + y_ref[:, :half],
                                    alpha * x_ref[:, half:] + y_ref[:, half:], g_ref[...], b_ref[...])
    o_ref[:, :half] = o_lo
    o_ref[:, half:] = o_hi
    op_ref[...] = _pack_bf16_pair(o_lo, o_hi)


def residual_layer_norm(x, y, g, b, layer, alpha, bt=256):
    s, d = x.shape
    bt = min(bt, s)
    row = pl.BlockSpec((bt, d), lambda i: (i, 0))
    par = pl.BlockSpec((None, 1, d), lambda i: (layer, 0, 0))
    return pl.pallas_call(
        functools.partial(_ln_kernel, alpha=alpha),
        out_shape=(jax.ShapeDtypeStruct((s, d), F32), jax.ShapeDtypeStruct((s, d // 2), U32)),
        grid=(s // bt,),
        in_specs=[row, row, par, par],
        out_specs=(row, pl.BlockSpec((bt, d // 2), lambda i: (i, 0))),
        compiler_params=_params(("parallel",), 2 * bt * d * (4 + 4 + 4 + 2)),
        name="residual_ln",
    )(x, y, g, b)


def _dn_prep_kernel(x_ref, halo_ref, w_ref, o_ref, xs_ref, *, bt, n_q_blocks, n_qk_blocks, q_scale):
    i = pl.program_id(0)
    j = pl.program_id(1)
    halo = halo_ref[...]
    xs_ref[0:V7X_SUBLANES, :] = jnp.where(i == 0, jnp.zeros_like(halo), halo)
    xs_ref[V7X_SUBLANES:, :] = x_ref[...]
    w = w_ref[...]
    y = None
    for tap in range(CONV_K):
        start = V7X_SUBLANES - (CONV_K - 1) + tap
        term = xs_ref[start:start + bt, :] * w[tap:tap + 1, :]
        y = term if y is None else y + term
    y = y * jax.nn.sigmoid(y)
    bc = y.shape[1]
    scale = jnp.where(j < n_q_blocks, q_scale, 1.0).astype(F32)
    for c in range(bc // DN_HEAD_DIM):
        sl = slice(c * DN_HEAD_DIM, (c + 1) * DN_HEAD_DIM)
        yh = y[:, sl]
        nrm = yh * (lax.rsqrt(jnp.sum(yh * yh, -1, keepdims=True) + L2_EPS) * scale)
        o_ref[:, sl] = jnp.where(j < n_qk_blocks, nrm, yh).astype(BF16)


def dn_prep(p1, conv_w, layer, dn_width, bt=512, bc=512):
    s = p1.shape[0]
    ncols = 3 * dn_width
    bt = min(bt, s)
    assert dn_width % bc == 0 and s % bt == 0
    hb = bt // V7X_SUBLANES
    return pl.pallas_call(
        functools.partial(_dn_prep_kernel, bt=bt, n_q_blocks=dn_width // bc, n_qk_blocks=2 * dn_width // bc,
                          q_scale=DN_HEAD_DIM ** -0.5),
        out_shape=jax.ShapeDtypeStruct((s, ncols), BF16),
        grid=(s // bt, ncols // bc),
        in_specs=[
            pl.BlockSpec((bt, bc), lambda i, j: (i, j)),
            pl.BlockSpec((V7X_SUBLANES, bc), lambda i, j: (jnp.maximum(i * hb - 1, 0), j)),
            pl.BlockSpec((None, CONV_K, bc), lambda i, j: (layer, 0, j)),
        ],
        out_specs=pl.BlockSpec((bt, bc), lambda i, j: (i, j)),
        scratch_shapes=[pltpu.VMEM((bt + V7X_SUBLANES, bc), F32)],
        compiler_params=_params(("parallel", "parallel"), 4 * bt * bc * 4),
        name="dn_prep",
    )(p1, p1, conv_w)


def _softplus(x):
    return jnp.maximum(x, 0.0) + jnp.log1p(jnp.exp(-jnp.abs(x)))


def _dot32(a, b):
    return jnp.dot(a, b, preferred_element_type=F32)


def _unit_lower_inverses(a_list, row, col):
    c = a_list[0].shape[0]
    hs = range(len(a_list))
    eye = (row == col).astype(F32)
    dist = row ^ col
    n = [jnp.where(dist < DN_INV_BASE, -a, 0.0) for a in a_list]
    t = [eye + n[h] for h in hs]
    p = [n[h].astype(BF16) for h in hs]
    for _ in range(int(math.log2(DN_INV_BASE)) - 1):
        p = [_dot32(p[h], p[h]).astype(BF16) for h in hs]
        t = [t[h] + _dot32(t[h].astype(BF16), p[h]) for h in hs]
    b = DN_INV_BASE
    while b < c:
        pair = (dist >= b) & (dist < 2 * b)
        mb = [jnp.where(pair, a, 0.0).astype(BF16) for a in a_list]
        tb = [t[h].astype(BF16) for h in hs]
        tm = [_dot32(tb[h], mb[h]).astype(BF16) for h in hs]
        t = [t[h] - _dot32(tm[h], tb[h]) for h in hs]
        b *= 2
    return t


def _dn_kernel(q_ref, k_ref, v_ref, z_ref, ac_ref, bc_ref, ar_ref, br_ref, alc_ref, dtc_ref, alr_ref, dtr_ref,
               nw_ref, o_ref, state_ref, *, hb, chunk, n_chunks):
    c_ = chunk
    d_ = DN_HEAD_DIM

    @pl.when(pl.program_id(1) == 0)
    def _reset():
        state_ref[...] = jnp.zeros_like(state_ref)

    row = lax.broadcasted_iota(I32, (c_, c_), 0)
    col = lax.broadcasted_iota(I32, (c_, c_), 1)
    incl = row >= col
    strict = row > col
    ltri = incl.astype(F32)
    neg_decay_c = -jnp.exp(alc_ref[...])
    neg_decay_r = -jnp.exp(alr_ref[...])
    nw = nw_ref[...]
    contract_last = (((1,), (1,)), ((), ()))
    contract_first = (((0,), (0,)), ((), ()))

    def chunk_body(ci, carry):
        r0 = pl.multiple_of(ci * c_, c_)
        rows = pl.ds(r0, c_)
        g_c = neg_decay_c * _softplus(ac_ref[rows, :] + dtc_ref[...])
        beta_c = jax.nn.sigmoid(bc_ref[rows, :])
        cum_c = jnp.dot(ltri, g_c, precision=HIGHEST, preferred_element_type=F32)
        g_r = neg_decay_r * _softplus(ar_ref[ci] + dtr_ref[...])
        cum_r = lax.dot_general(g_r, ltri, contract_last, precision=HIGHEST, preferred_element_type=F32)
        hs = range(hb)
        cols = [slice(hh * d_, (hh + 1) * d_) for hh in hs]
        gc = [cum_c[:, hh:hh + 1] for hh in hs]
        gr = [cum_r[hh:hh + 1, :] for hh in hs]
        bcol = [beta_c[:, hh:hh + 1] for hh in hs]
        qh = [q_ref[rows, cols[hh]] for hh in hs]
        kh = [k_ref[rows, cols[hh]] for hh in hs]
        vh = [v_ref[rows, cols[hh]] for hh in hs]
        kk = [lax.dot_general(kh[hh], kh[hh], contract_last, preferred_element_type=F32) for hh in hs]
        qk = [lax.dot_general(qh[hh], kh[hh], contract_last, preferred_element_type=F32) for hh in hs]
        decay = [jnp.exp(jnp.where(incl, gc[hh] - gr[hh], -jnp.inf)) for hh in hs]
        a = [jnp.where(strict, bcol[hh] * kk[hh] * decay[hh], 0.0) for hh in hs]
        t = _unit_lower_inverses(a, row, col)
        e_g = [jnp.exp(gc[hh]) for hh in hs]
        kf = [kh[hh].astype(F32) for hh in hs]
        rhs = [jnp.concatenate([vh[hh].astype(F32) * bcol[hh], kf[hh] * (bcol[hh] * e_g[hh])], axis=1).astype(BF16)
               for hh in hs]
        uw = [_dot32(t[hh].astype(BF16), rhs[hh]) for hh in hs]
        state = [state_ref[hh] for hh in hs]
        wq = [jnp.concatenate([uw[hh][:, d_:].astype(BF16), (qh[hh].astype(F32) * e_g[hh]).astype(BF16)], axis=0)
              for hh in hs]
        ws = [_dot32(wq[hh], state[hh].astype(BF16)) for hh in hs]
        v_new = [(uw[hh][:, :d_] - ws[hh][:c_]).astype(BF16) for hh in hs]
        o = [ws[hh][c_:] + _dot32((qk[hh] * decay[hh]).astype(BF16), v_new[hh]) for hh in hs]
        g_last = [gc[hh][c_ - 1:c_, :] for hh in hs]
        k_dec = [(kf[hh] * jnp.exp(g_last[hh] - gc[hh])).astype(BF16) for hh in hs]
        for hh in hs:
            state_ref[hh] = state[hh] * jnp.exp(g_last[hh]) + lax.dot_general(
                k_dec[hh], v_new[hh], contract_first, preferred_element_type=F32)
        for hh in hs:
            zf = z_ref[rows, cols[hh]]
            on = o[hh] * lax.rsqrt(jnp.mean(o[hh] * o[hh], -1, keepdims=True) + RMS_EPS) * nw
            o_ref[rows, cols[hh]] = (on * (zf * jax.nn.sigmoid(zf))).astype(BF16)
        return carry

    lax.fori_loop(0, n_chunks, chunk_body, 0)


def deltanet(qkv, p1, gates_col, gates_row, a_log, dt_bias, norm_w, layer, dn_width, bs=512):
    s = qkv.shape[0]
    hb = DN_HEADS_PER_STEP
    c_ = DN_CHUNK
    heads = dn_width // DN_HEAD_DIM
    assert heads % hb == 0
    groups = heads // hb
    bs = min(bs, s)
    assert s % bs == 0 and bs % c_ == 0
    wblk = hb * DN_HEAD_DIM
    ncb = dn_width // wblk
    n_chunks = bs // c_
    a_log_c = a_log.reshape(-1, groups, 1, hb)
    a_log_r = a_log.reshape(-1, groups, hb, 1)
    dt_c = dt_bias.reshape(-1, groups, 1, hb)
    dt_r = dt_bias.reshape(-1, groups, hb, 1)
    nw = norm_w.reshape(-1, 1, DN_HEAD_DIM)

    def seg(which):
        return pl.BlockSpec((bs, wblk), lambda g, i: (i, which * ncb + g))

    col_spec = lambda w: pl.BlockSpec((None, None, bs, hb), lambda g, i: (w, g, i, 0))
    row_spec = lambda w: pl.BlockSpec((None, None, n_chunks, hb, c_), lambda g, i: (w, g, i, 0, 0))
    pc_spec = pl.BlockSpec((None, None, 1, hb), lambda g, i: (layer, g, 0, 0))
    pr_spec = pl.BlockSpec((None, None, hb, 1), lambda g, i: (layer, g, 0, 0))
    vmem = 2 * bs * wblk * (2 * 3 + 4 + 2) + hb * DN_HEAD_DIM * DN_HEAD_DIM * 4 + 16 * bs * V7X_LANES * 4
    return pl.pallas_call(
        functools.partial(_dn_kernel, hb=hb, chunk=c_, n_chunks=n_chunks),
        out_shape=jax.ShapeDtypeStruct((s, dn_width), BF16),
        grid=(groups, s // bs),
        in_specs=[seg(0), seg(1), seg(2), seg(3),
                  col_spec(0), col_spec(1), row_spec(0), row_spec(1),
                  pc_spec, pc_spec, pr_spec, pr_spec,
                  pl.BlockSpec((None, 1, DN_HEAD_DIM), lambda g, i: (layer, 0, 0))],
        out_specs=pl.BlockSpec((bs, wblk), lambda g, i: (i, g)),
        scratch_shapes=[pltpu.VMEM((hb, DN_HEAD_DIM, DN_HEAD_DIM), F32)],
        compiler_params=_params(("parallel", "arbitrary"), vmem),
        name="deltanet",
    )(qkv, qkv, qkv, p1, gates_col, gates_col, gates_row, gates_row, a_log_c, dt_c, a_log_r, dt_r, nw)


def _da_kernel(lam_init_ref, q_ref, k_ref, v_ref, lq1_ref, lk1_ref, lq2_ref, lk2_ref, nw_ref, o_ref,
               acc_ref, m_ref, l_ref, s_a, s_b, mx_a, mx_b, *, blk):
    iq = pl.program_id(1)
    d_ = DA_HEAD_DIM
    rc = DA_ROW_CHUNK
    contract_last = (((1,), (1,)), ((), ()))
    m_ref[...] = jnp.full_like(m_ref, -jnp.inf)
    l_ref[...] = jnp.zeros_like(l_ref)
    acc_ref[...] = jnp.zeros_like(acc_ref)
    chains = [(c, sub) for c in range(blk // rc) for sub in range(2)]
    rsl = lambda c: slice(c * rc, (c + 1) * rc)
    lanes = lambda x, width: jnp.concatenate([x] * (width // V7X_LANES), axis=1)

    def scores(j, out):
        s_out, mx_out = out
        kb = k_ref[pl.ds(pl.multiple_of(j * blk, blk), blk), :]
        for sub in range(2):
            s_big = lax.dot_general(q_ref[:, sub * d_:(sub + 1) * d_], kb[:, sub * d_:(sub + 1) * d_],
                                    contract_last, preferred_element_type=F32)
            s_out[sub] = s_big
            mx_out[sub] = jnp.broadcast_to(jnp.max(s_big, -1, keepdims=True), (blk, V7X_LANES))

    def kv_step(j, cur, nxt):
        scores(j + 1, nxt)
        s_cur, mx_cur = cur
        vb = v_ref[pl.ds(pl.multiple_of(j * blk, blk), blk), :]
        s = [s_cur[sub, rsl(c), :] for c, sub in chains]
        m_cur = [mx_cur[sub, rsl(c), :] for c, sub in chains]
        alpha, pb = [], []
        for n, (c, sub) in enumerate(chains):
            m_old = m_ref[sub, rsl(c), :]
            m_new = jnp.maximum(m_old, m_cur[n])
            a_n = jnp.exp2(m_old - m_new)
            p = jnp.exp2(s[n] - lanes(m_new, blk))
            m_ref[sub, rsl(c), :] = m_new
            l_ref[sub, rsl(c), :] = a_n * l_ref[sub, rsl(c), :] + jnp.sum(p, -1, keepdims=True)
            alpha.append(a_n)
            pb.append(p.astype(BF16))
        for sub in range(2):
            mine = [n for n, (c, sb) in enumerate(chains) if sb == sub]
            p_sub = jnp.concatenate([pb[n] for n in mine], axis=0)
            a_sub = jnp.concatenate([alpha[n] for n in mine], axis=0)
            acc_ref[sub] = lanes(a_sub, DA_V_DIM) * acc_ref[sub] + _dot32(p_sub, vb)

    def diag_step(cur):
        s_cur, _ = cur
        r0 = pl.multiple_of(iq * blk, blk)
        for n, (c, sub) in enumerate(chains):
            kw = (c + 1) * rc
            row = lax.broadcasted_iota(I32, (rc, kw), 0)
            col = lax.broadcasted_iota(I32, (rc, kw), 1)
            s_n = jnp.where(col <= row + c * rc, s_cur[sub, rsl(c), :kw], -jnp.inf)
            m_old = m_ref[sub, rsl(c), :]
            m_new = jnp.maximum(m_old, jnp.max(s_n, -1, keepdims=True))
            a_n = jnp.exp2(m_old - m_new)
            p = jnp.exp2(s_n - lanes(m_new, kw))
            m_ref[sub, rsl(c), :] = m_new
            l_ref[sub, rsl(c), :] = a_n * l_ref[sub, rsl(c), :] + jnp.sum(p, -1, keepdims=True)
            acc_ref[sub, rsl(c), :] = (lanes(a_n, DA_V_DIM) * acc_ref[sub, rsl(c), :]
                                       + _dot32(p.astype(BF16), v_ref[pl.ds(r0, kw), :]))

    odd = lax.rem(iq, 2)
    buf_a, buf_b = (s_a, mx_a), (s_b, mx_b)

    @pl.when(odd == 0)
    def _even_start():
        scores(0, buf_a)

    @pl.when(odd == 1)
    def _odd_start():
        scores(0, buf_b)
        kv_step(0, buf_b, buf_a)

    def pair(jj, carry):
        j = odd + 2 * jj
        kv_step(j, buf_a, buf_b)
        kv_step(j + 1, buf_b, buf_a)
        return carry

    lax.fori_loop(0, (iq - odd) // 2, pair, 0)
    diag_step(buf_a)

    lam_init = lam_init_ref[0]
    lam = (jnp.exp(jnp.sum(lq1_ref[...] * lk1_ref[...], keepdims=True))
           - jnp.exp(jnp.sum(lq2_ref[...] * lk2_ref[...], keepdims=True)) + lam_init)
    o = acc_ref[0] / lanes(l_ref[0], DA_V_DIM) - lam * (acc_ref[1] / lanes(l_ref[1], DA_V_DIM))
    on = o * lax.rsqrt(jnp.mean(o * o, -1, keepdims=True) + RMS_EPS) * nw_ref[...]
    o_ref[...] = (on * (1.0 - lam_init)).astype(BF16)


def diff_attention(p2, width, lq1, lk1, lq2, lk2, norm_w, lam_init, layer, blk=512):
    s = p2.shape[0]
    heads = width // DA_V_DIM
    blk = min(blk, s)
    assert s % blk == 0
    vec = pl.BlockSpec((None, 1, DA_HEAD_DIM), lambda h, i, *_: (layer, 0, 0))
    full = lambda seg: pl.BlockSpec((s, DA_V_DIM), lambda h, i, *_: (0, seg * heads + h))
    tile = pl.BlockSpec((blk, DA_V_DIM), lambda h, i, *_: (i, h))
    grid_spec = pltpu.PrefetchScalarGridSpec(
        num_scalar_prefetch=1,
        grid=(heads, s // blk),
        in_specs=[tile, full(1), full(2), vec, vec, vec, vec,
                  pl.BlockSpec((None, 1, DA_V_DIM), lambda h, i, *_: (layer, 0, 0))],
        out_specs=tile,
        scratch_shapes=[pltpu.VMEM((2, blk, DA_V_DIM), F32), pltpu.VMEM((2, blk, V7X_LANES), F32),
                        pltpu.VMEM((2, blk, V7X_LANES), F32),
                        pltpu.VMEM((2, blk, blk), F32), pltpu.VMEM((2, blk, blk), F32),
                        pltpu.VMEM((2, blk, V7X_LANES), F32), pltpu.VMEM((2, blk, V7X_LANES), F32)],
    )
    vmem = 4 * s * DA_V_DIM * 2 + 4 * blk * DA_V_DIM * 2 + 2 * blk * DA_V_DIM * 4 + 8 * blk * blk * 4
    return pl.pallas_call(
        functools.partial(_da_kernel, blk=blk),
        out_shape=jax.ShapeDtypeStruct((s, width), BF16),
        grid_spec=grid_spec,
        compiler_params=_params(("parallel", "arbitrary"), vmem),
        name="diff_attention",
    )(lam_init, p2, p2, p2, lq1, lk1, lq2, lk2, norm_w)


def _mem_attn_kernel(q_ref, k_ref, v_ref, o_ref, *, scale):
    q = (q_ref[...].astype(F32) * scale).astype(BF16)
    k = k_ref[...].astype(BF16)
    sc = lax.dot_general(q, k, (((1,), (1,)), ((), ())), preferred_element_type=F32)
    sc = sc - jnp.max(sc, -1, keepdims=True)
    p = jnp.exp(sc)
    denom = jnp.sum(p, -1, keepdims=True)
    o = jnp.dot(p.astype(BF16), v_ref[...].astype(BF16), preferred_element_type=F32) / denom
    o_ref[...] = o.astype(BF16)


def memory_attention(p2, mem_kv, q_col0, mem_width, bt=512):
    s = p2.shape[0]
    n_mem = mem_kv.shape[0]
    hd = mem_width // MEM_HEADS
    bt = min(bt, s)
    assert q_col0 % hd == 0
    qb0 = q_col0 // hd
    return pl.pallas_call(
        functools.partial(_mem_attn_kernel, scale=hd ** -0.5),
        out_shape=jax.ShapeDtypeStruct((s, mem_width), BF16),
        grid=(s // bt, MEM_HEADS),
        in_specs=[pl.BlockSpec((bt, hd), lambda i, h: (i, qb0 + h)),
                  pl.BlockSpec((n_mem, hd), lambda i, h: (0, h)),
                  pl.BlockSpec((n_mem, hd), lambda i, h: (0, MEM_HEADS + h))],
        out_specs=pl.BlockSpec((bt, hd), lambda i, h: (i, h)),
        compiler_params=_params(("parallel", "parallel"), 4 * bt * hd * 4 + 4 * n_mem * hd * 4 + 4 * bt * n_mem * 4),
        name="memory_attention",
    )(p2, mem_kv, mem_kv)


def _router_kernel(x_ref, w_ref, b_ref, idx_ref, gate_ref, rank_ref, cnt_ref, carry_ref):
    i = pl.program_id(0)

    @pl.when(i == 0)
    def _reset():
        carry_ref[...] = jnp.zeros_like(carry_ref)

    x = x_ref[...]
    w = w_ref[...]
    x_hi = x.astype(BF16)
    w_hi = w.astype(BF16)
    x_lo = (x - x_hi.astype(F32)).astype(BF16)
    w_lo = (w - w_hi.astype(F32)).astype(BF16)
    logits = _dot32(x_hi, w_hi) + (_dot32(x_lo, w_hi) + _dot32(x_hi, w_lo)) + b_ref[...]
    bt, n_e = logits.shape
    lane = lax.broadcasted_iota(I32, (bt, n_e), 1)
    work = logits
    vals, sels = [], []
    for kk in range(TOP_K):
        mx = jnp.max(work, -1, keepdims=True)
        idx = jnp.min(jnp.where(work == mx, lane, n_e), -1, keepdims=True)
        sel = lane == idx
        vals.append(mx)
        sels.append(sel)
        idx_ref[:, kk:kk + 1] = idx
        work = jnp.where(sel, -jnp.inf, work)
    exps = [jnp.exp(v - vals[0]) for v in vals]
    denom = exps[0]
    for e in exps[1:]:
        denom = denom + e
    for kk in range(TOP_K):
        gate_ref[:, kk:kk + 1] = exps[kk] / denom
    mask = sels[0]
    for sel in sels[1:]:
        mask = mask | sel
    maskf = mask.astype(F32)
    row = lax.broadcasted_iota(I32, (bt, bt), 0)
    col = lax.broadcasted_iota(I32, (bt, bt), 1)
    before = (row > col).astype(BF16)
    rank = carry_ref[...] + jnp.dot(before, maskf.astype(BF16), preferred_element_type=F32)
    for kk in range(TOP_K):
        rank_ref[:, kk:kk + 1] = jnp.sum(jnp.where(sels[kk], rank, 0.0), -1, keepdims=True).astype(I32)
    carry_ref[...] = carry_ref[...] + jnp.sum(maskf, 0, keepdims=True)
    cnt_ref[...] = carry_ref[...].astype(I32)


def router(x, router_w, router_b, layer, bt=256):
    s, d = x.shape
    n_e = router_w.shape[-1]
    bt = min(bt, s)
    small = lambda dt: jax.ShapeDtypeStruct((s, TOP_K), dt)
    tok = pl.BlockSpec((bt, TOP_K), lambda i: (i, 0))
    return pl.pallas_call(
        _router_kernel,
        out_shape=(small(I32), small(F32), small(I32), jax.ShapeDtypeStruct((1, n_e), I32)),
        grid=(s // bt,),
        in_specs=[pl.BlockSpec((bt, d), lambda i: (i, 0)),
                  pl.BlockSpec((None, d, n_e), lambda i: (layer, 0, 0)),
                  pl.BlockSpec((None, 1, n_e), lambda i: (layer, 0, 0))],
        out_specs=(tok, tok, tok, pl.BlockSpec((1, n_e), lambda i: (0, 0))),
        scratch_shapes=[pltpu.VMEM((1, n_e), F32)],
        compiler_params=_params(("arbitrary",), 2 * bt * d * 4 + 2 * d * V7X_LANES * 4),
        name="router",
    )(x, router_w, router_b)


def _row_copy(src, src_row, dst, dst_row, sem):
    return pltpu.make_async_copy(src.at[pl.ds(src_row, 1)], dst.at[pl.ds(dst_row, 1)], sem)


def _moe_kernel(te_ref, nt_ref, src_next_ref, dst_prev_ref, dst_ref, src_ref, x_hbm, wg_ref, wu_ref, wd_ref,
                bg_ref, bu_ref, bd_ref, y_hbm, xbuf, ybuf, wgb, wub, wdb, gsem, ssem, *, tm):
    i = pl.program_id(0)
    nt = nt_ref[0]
    slot = lax.rem(i, 2)
    other = 1 - slot
    yslot = lax.rem(i, 3)
    yprev = lax.rem(i + 2, 3)
    ynext = lax.rem(i + 1, 3)
    groups = MOE_DMA_GROUPS
    bounds = [round(g * tm / groups) for g in range(groups + 1)]

    def start_gather(idx_ref, s_, g):
        for r in range(bounds[g], bounds[g + 1]):
            _row_copy(x_hbm, idx_ref[0, r], xbuf.at[s_], r, gsem.at[s_]).start(priority=r % 2)

    def start_scatter(idx_ref, s_, g):
        for r in range(bounds[g], bounds[g + 1]):
            _row_copy(ybuf.at[s_], r, y_hbm, idx_ref[0, r], ssem.at[s_]).start(priority=r % 2)

    def wait_gather(s_):
        pltpu.make_async_copy(x_hbm.at[pl.ds(0, tm)], xbuf.at[s_], gsem.at[s_]).wait()

    def wait_scatter(s_):
        pltpu.make_async_copy(ybuf.at[s_], y_hbm.at[pl.ds(0, tm)], ssem.at[s_]).wait()

    @pl.when(i == 0)
    def _prologue():
        ybuf[yprev] = jnp.zeros((tm, ybuf.shape[2]), U32)
        for region in range(2):
            first = y_hbm.shape[0] - MOE_SPARE_REGIONS * tm + region * tm
            spare = pltpu.make_async_copy(ybuf.at[yprev], y_hbm.at[pl.ds(first, tm)], ssem.at[yprev])
            spare.start()
            spare.wait()
        for g in range(groups):
            start_gather(src_ref, slot, g)

    @pl.when(i < nt)
    def _tile():
        wait_gather(slot)
        prev = te_ref[jnp.maximum(i - 1, 0)]

        @pl.when((i == 0) | (te_ref[i] != prev))
        def _cast_weights():
            wgb[...] = wg_ref[...].astype(BF16)
            wub[...] = wu_ref[...].astype(BF16)
            wdb[...] = wd_ref[...].astype(BF16)

        half = xbuf.shape[2]
        x_lo, x_hi = _unpack_bf16_pair(xbuf[slot])
        x_lo, x_hi = x_lo.astype(BF16), x_hi.astype(BF16)
        gate = _dot32(x_lo, wgb[:half, :])
        start_gather(src_next_ref, other, 0)
        up = _dot32(x_lo, wub[:half, :])
        start_gather(src_next_ref, other, 1)
        gate = gate + _dot32(x_hi, wgb[half:, :]) + bg_ref[...]
        start_gather(src_next_ref, other, 2)
        up = up + _dot32(x_hi, wub[half:, :]) + bu_ref[...]
        start_scatter(dst_prev_ref, yprev, 0)
        gate = jnp.minimum(gate, SWIGLU_LIMIT)
        up = jnp.clip(up, -SWIGLU_LIMIT, SWIGLU_LIMIT)
        act = ((up + 1.0) * gate * jax.nn.sigmoid(SWIGLU_ALPHA * gate)).astype(BF16)
        y_lo = _dot32(act, wdb[:, :half]) + bd_ref[:, :half]
        start_scatter(dst_prev_ref, yprev, 1)
        y_hi = _dot32(act, wdb[:, half:]) + bd_ref[:, half:]
        start_scatter(dst_prev_ref, yprev, 2)
        ybuf[yslot] = _pack_bf16_pair(y_lo, y_hi)

        @pl.when(i > 0)
        def _drain_older():
            wait_scatter(ynext)

        @pl.when(i == nt - 1)
        def _drain_last():
            wait_gather(other)
            wait_scatter(yprev)
            for g in range(groups):
                start_scatter(dst_ref, yslot, g)
            wait_scatter(yslot)


def routed_experts(x, tile_expert, n_tiles, src_tok, dst_row, w_gate, b_gate, w_up, b_up, w_down, b_down, layer):
    s, half = x.shape
    d = 2 * half
    f = w_gate.shape[-1]
    tm = MOE_TILE
    nt = tile_expert.shape[0]
    idx_spec = pl.BlockSpec((None, 1, tm), lambda i, *_: (i, 0, 0), memory_space=pltpu.SMEM)
    idx_next = pl.BlockSpec((None, 1, tm), lambda i, *_: (jnp.minimum(i + 1, nt - 1), 0, 0),
                            memory_space=pltpu.SMEM)
    idx_prev = pl.BlockSpec((None, 1, tm), lambda i, *_: (i, 0, 0), memory_space=pltpu.SMEM)
    idx_own = pl.BlockSpec((None, 1, tm), lambda i, *_: (i + 1, 0, 0), memory_space=pltpu.SMEM)
    wspec = lambda a, b: pl.BlockSpec((None, None, a, b), lambda i, te, n: (layer, te[i], 0, 0))
    grid_spec = pltpu.PrefetchScalarGridSpec(
        num_scalar_prefetch=2,
        grid=(nt,),
        in_specs=[idx_next, idx_prev, idx_own, idx_spec, pl.BlockSpec(memory_space=pl.ANY),
                  wspec(d, f), wspec(d, f), wspec(f, d), wspec(1, f), wspec(1, f), wspec(1, d)],
        out_specs=pl.BlockSpec(memory_space=pl.ANY),
        scratch_shapes=[pltpu.VMEM((2, tm, half), U32), pltpu.VMEM((3, tm, half), U32),
                        pltpu.VMEM((d, f), BF16), pltpu.VMEM((d, f), BF16), pltpu.VMEM((f, d), BF16),
                        pltpu.SemaphoreType.DMA((2,)), pltpu.SemaphoreType.DMA((3,))],
    )
    vmem = 2 * 3 * d * f * 4 + 3 * d * f * 2 + 5 * tm * half * 4 + 4 * tm * d * 4
    return pl.pallas_call(
        functools.partial(_moe_kernel, tm=tm),
        out_shape=jax.ShapeDtypeStruct((TOP_K * s + MOE_SPARE_REGIONS * tm, half), U32),
        grid_spec=grid_spec,
        compiler_params=_params(("arbitrary",), vmem),
        name="routed_experts",
    )(tile_expert, n_tiles, src_tok, dst_row, dst_row, src_tok, x, w_gate, w_up, w_down, b_gate, b_up, b_down)


def _combine_ln_kernel(x_ref, y0_ref, y1_ref, y2_ref, y3_ref, gate_ref, g_ref, b_ref, o_ref, ob_ref, *, alpha):
    half = y0_ref.shape[1]
    gates = gate_ref[...]
    ffn_lo = ffn_hi = None
    for kk, y_ref in enumerate((y0_ref, y1_ref, y2_ref, y3_ref)):
        y_lo, y_hi = _unpack_bf16_pair(y_ref[...])
        gk = gates[:, kk:kk + 1]
        ffn_lo = gk * y_lo if ffn_lo is None else ffn_lo + gk * y_lo
        ffn_hi = gk * y_hi if ffn_hi is None else ffn_hi + gk * y_hi
    o_lo, o_hi = _layer_norm_halves(alpha * x_ref[:, :half] + ffn_lo, alpha * x_ref[:, half:] + ffn_hi,
                                    g_ref[...], b_ref[...])
    o_ref[:, :half] = o_lo
    o_ref[:, half:] = o_hi
    ob_ref[:, :half] = o_lo.astype(BF16)
    ob_ref[:, half:] = o_hi.astype(BF16)


def combine_layer_norm(x, y, gates, g, b, layer, alpha, bt=256):
    s, d = x.shape
    half = d // 2
    bt = min(bt, s)
    nb = s // bt
    row = pl.BlockSpec((bt, d), lambda i: (i, 0))
    yk = lambda kk: pl.BlockSpec((bt, half), lambda i: (kk * nb + i, 0))
    par = pl.BlockSpec((None, 1, d), lambda i: (layer, 0, 0))
    return pl.pallas_call(
        functools.partial(_combine_ln_kernel, alpha=alpha),
        out_shape=(jax.ShapeDtypeStruct((s, d), F32), jax.ShapeDtypeStruct((s, d), BF16)),
        grid=(nb,),
        in_specs=[row, yk(0), yk(1), yk(2), yk(3), pl.BlockSpec((bt, TOP_K), lambda i: (i, 0)), par, par],
        out_specs=(row, row),
        compiler_params=_params(("parallel",), 2 * bt * d * (4 + 4 * 2 + 4 + 2)),
        name="combine_ln",
    )(x, y, y, y, y, gates, g, b)


def _routing_plan(idx, rank, counts, s, n_e):
    tm = MOE_TILE
    nt = s * TOP_K // tm + n_e
    counts = counts.reshape(n_e)
    tiles_e = (counts + tm - 1) // tm
    tile_end = jnp.cumsum(tiles_e)
    tile_start = tile_end - tiles_e
    n_tiles = tile_end[-1]
    tile_ids = jnp.arange(nt, dtype=I32)
    te = jnp.sum((tile_end[None, :] <= jnp.minimum(tile_ids, n_tiles - 1)[:, None]).astype(I32), axis=1)
    te = jnp.clip(te, 0, n_e - 1).astype(I32)
    pos = (tile_start * tm)[idx] + rank
    tok = jnp.arange(s, dtype=I32)[:, None]
    val = jnp.arange(TOP_K, dtype=I32)[None, :] * s + tok
    row_in_tile = jnp.arange(tm, dtype=I32)
    spare = TOP_K * s + (tile_ids % 2)[:, None] * tm + row_in_tile[None, :]
    slots = spare.reshape(-1).at[pos.reshape(-1)].set(val.reshape(-1), unique_indices=True)
    src_tok = (slots % s).reshape(nt, 1, tm)
    zero_tile_rows = TOP_K * s + 2 * tm + row_in_tile
    dst_row = jnp.concatenate([zero_tile_rows, slots]).reshape(nt + 1, 1, tm)
    return te, n_tiles.reshape(1).astype(I32), src_tok, dst_row


def kernel(x, mem, positions, w_in, conv_w, a_log, dt_bias, dn_norm_w, lambda_q1, lambda_k1, lambda_q2,
           lambda_k2, da_norm_w, w_mem_kv, w_o, ln1_g, ln1_b, router_w, router_b, w_gate, b_gate, w_up, b_up,
           w_down, b_down, ln2_g, ln2_b):
    bsz, s, d = x.shape
    assert bsz == 1
    depth = w_in.shape[0]
    n_e = router_w.shape[-1]
    f = w_gate.shape[-1]
    dn_w = 3 * d // 8
    da_w = 3 * d // 8
    mem_w = d - dn_w - da_w
    dn_heads = dn_w // DN_HEAD_DIM
    alpha = (2 * depth) ** 0.25
    hb = DN_HEADS_PER_STEP
    groups = dn_heads // hb
    c_ = DN_CHUNK

    inv = 1.0 / (ROPE_THETA ** (jnp.arange(0, DA_HEAD_DIM, 2, dtype=F32) / DA_HEAD_DIM))
    ang = positions[0].astype(F32)[:, None] * inv
    ang = jnp.concatenate([ang, ang], -1)
    sign = jnp.concatenate([-jnp.ones((DA_HEAD_DIM // 2,), F32), jnp.ones((DA_HEAD_DIM // 2,), F32)])
    cos, sin_signed = jnp.cos(ang), jnp.sin(ang) * sign
    gate0 = 4 * dn_w
    post0 = gate0 + 2 * dn_heads
    w_in_t = jnp.swapaxes(w_in, 1, 2)
    lam_inits = [0.8 - 0.6 * math.exp(-0.3 * layer) for layer in range(depth)]
    vec3 = lambda p: p.reshape(depth, 1, -1)
    ln1_g3, ln1_b3, ln2_g3, ln2_b3 = vec3(ln1_g), vec3(ln1_b), vec3(ln2_g), vec3(ln2_b)
    lq1, lk1, lq2, lk2 = vec3(lambda_q1), vec3(lambda_k1), vec3(lambda_q2), vec3(lambda_k2)
    da_nw = vec3(da_norm_w)
    router_b3 = vec3(router_b)
    b_gate4 = b_gate.reshape(depth, n_e, 1, f)
    b_up4 = b_up.reshape(depth, n_e, 1, f)
    b_down4 = b_down.reshape(depth, n_e, 1, d)

    xf = x[0]
    xb = cast_bf16(xf)
    memb = cast_bf16(mem[0])
    for layer in range(depth):
        p1 = matmul_nt(xb, w_in_t, layer, 4 * dn_w, name="proj_dn")
        rope = (cos, sin_signed, 2 * da_w, da_w, DA_HEAD_DIM ** -0.5 * LOG2E)
        p2 = matmul_nt(xb, w_in_t, layer, 3 * da_w + mem_w, row0=post0, out_dtype=BF16, rope=rope,
                       name="proj_attn")
        gts = matmul_nt(xb, w_in_t, layer, V7X_LANES, row0=gate0, bn=V7X_LANES, name="proj_gates")
        mem_kv = matmul([memb], w_mem_kv, layer, 2 * mem_w, name="proj_mem")

        ab = gts[:, :2 * dn_heads].reshape(s, 2, groups, hb)
        gates_col = ab.transpose(1, 2, 0, 3)
        gates_row = ab.reshape(s // c_, c_, 2, groups, hb).transpose(2, 3, 0, 4, 1)
        qkv = dn_prep(p1, conv_w, layer, dn_w)
        y_dn = deltanet(qkv, p1, gates_col, gates_row, a_log, dt_bias, dn_norm_w, layer, dn_w)

        lam_init = jnp.full((1,), lam_inits[layer], F32)
        y_da = diff_attention(p2, da_w, lq1, lk1, lq2, lk2, da_nw, lam_init, layer)
        y_mem = memory_attention(p2, mem_kv, 3 * da_w, mem_w)

        mixed = matmul([y_dn, y_da, y_mem], w_o, layer, d, name="proj_out")
        x1, x1_packed = residual_layer_norm(xf, mixed, ln1_g3, ln1_b3, layer, alpha)

        idx, gates, rank, counts = router(x1, router_w, router_b3, layer)
        te, n_tiles, src_tok, dst_row = _routing_plan(idx, rank, counts, s, n_e)
        y = routed_experts(x1_packed, te, n_tiles, src_tok, dst_row, w_gate, b_gate4, w_up, b_up4, w_down, b_down4,
                           layer)
        xf, xb = combine_layer_norm(x1, y, gates, ln2_g3, ln2_b3, layer, alpha)
    return xf[None]
```
